```python
import math
import jax, jax.numpy as jnp
from jax import lax
import numpy as np

D_MODEL = 2048
BATCH = 2
SEQ = 4096
DEPTH = 1

N_META = 16
ATTN_HEADS = 16
ATTN_KV_HEADS = 4
HEAD_DIM = 128
IDX_HEADS = 16
IDX_DIM = 64
TOPK_MAX = 256
Q_BLOCK = 128
ROPE_THETA = 10000.0
SSM_EXPAND = 2
D_INNER = SSM_EXPAND * D_MODEL
SSM_HEAD_DIM = 64
SSM_HEADS = D_INNER // SSM_HEAD_DIM
SSM_GROUPS = 8
D_STATE = 128
CONV_WIDTH = 4
CHUNK = 256
CONV_DIM = D_INNER + 2 * SSM_GROUPS * D_STATE
PEER_HEADS = 8
N_KEYS = 128
N_EXPERTS = N_KEYS * N_KEYS
PEER_KEY_DIM = 256
PEER_TOPK = 16
PEER_TOK_BLOCK = 128
NORM_EPS = 1e-6

W_Q = ATTN_HEADS * HEAD_DIM
W_KV = ATTN_KV_HEADS * HEAD_DIM
W_IQ = IDX_HEADS * IDX_DIM
W_IK = IDX_DIM
W_IW = IDX_HEADS
W_BC = SSM_GROUPS * D_STATE
W_GATE = 2 * D_MODEL
IN_WIDTH = W_Q + 2 * W_KV + W_IQ + W_IK + W_IW + 2 * D_INNER + 2 * W_BC + SSM_HEADS + W_GATE

kernel_name = 'hybrid_dsa_ssd_peer_block'


def rms_norm(x, w):
    xf = x.astype(jnp.float32)
    y = xf * lax.rsqrt(jnp.mean(xf * xf, axis=-1, keepdims=True) + NORM_EPS)
    return (y * w.astype(jnp.float32)).astype(x.dtype)


def rope(x, pos):
    d = x.shape[-1]
    half = d // 2
    inv = ROPE_THETA ** (-jnp.arange(half, dtype=jnp.float32) / half)
    ang = pos.astype(jnp.float32)[:, None] * inv[None, :]
    cos = jnp.cos(ang)[:, None, :]
    sin = jnp.sin(ang)[:, None, :]
    x1 = x[..., :half].astype(jnp.float32)
    x2 = x[..., half:].astype(jnp.float32)
    out = jnp.concatenate([x1 * cos - x2 * sin, x2 * cos + x1 * sin], axis=-1)
    return out.astype(x.dtype)


def dsa_attention(q, k, v, q_idx, k_idx, w_idx, topk):
    B, T, H, Dh = q.shape
    G = k.shape[2]
    rep = H // G
    n_blk = -(-T // Q_BLOCK)
    pad = n_blk * Q_BLOCK - T

    def to_blocks(a):
        a = jnp.pad(a, [(0, 0), (0, pad)] + [(0, 0)] * (a.ndim - 2))
        return a.reshape((B, n_blk, Q_BLOCK) + a.shape[2:]).swapaxes(0, 1)

    starts = jnp.arange(n_blk) * Q_BLOCK
    key_pos = jnp.arange(T)
    k_idx_f = k_idx.astype(jnp.float32)
    gather = jax.vmap(lambda a, i: a[i])

    def one_block(args):
        qb, qib, wib, start = args
        q_pos = start + jnp.arange(Q_BLOCK)
        logits = jnp.einsum('bqhd,bkd->bqhk', qib.astype(jnp.float32), k_idx_f) * (IDX_DIM ** -0.5)
        score = jnp.einsum('bqh,bqhk->bqk', wib.astype(jnp.float32) * (IDX_HEADS ** -0.5), jax.nn.relu(logits))
        admissible = key_pos[None, :] <= q_pos[:, None]
        score = jnp.where(admissible[None], score, -jnp.inf)
        _, sel = lax.top_k(score, topk)
        k_sel = gather(k, sel).astype(jnp.float32)
        v_sel = gather(v, sel).astype(jnp.float32)
        qg = qb.reshape(B, Q_BLOCK, G, rep, Dh).astype(jnp.float32)
        s = jnp.einsum('bqgrd,bqkgd->bqgrk', qg, k_sel) * (Dh ** -0.5)
        valid = sel <= q_pos[None, :, None]
        s = jnp.where(valid[:, :, None, None, :], s, -jnp.inf)
        p = jax.nn.softmax(s, axis=-1)
        o = jnp.einsum('bqgrk,bqkgd->bqgrd', p, v_sel)
        return o.reshape(B, Q_BLOCK, H * Dh).astype(q.dtype)

    out = lax.map(one_block, (to_blocks(q), to_blocks(q_idx), to_blocks(w_idx), starts))
    return out.swapaxes(0, 1).reshape(B, n_blk * Q_BLOCK, H * Dh)[:, :T]


def causal_dwconv(u, w, b):
    out = lax.conv_general_dilated(
        u, w[:, None, :].astype(u.dtype), window_strides=(1,), padding=[(CONV_WIDTH - 1, 0)],
        dimension_numbers=('NWC', 'WIO', 'NWC'), feature_group_count=u.shape[-1])
    return out + b.astype(u.dtype)


def segsum_exp(a_cs):
    l = a_cs.shape[-1]
    diff = a_cs[..., :, None] - a_cs[..., None, :]
    mask = jnp.tril(jnp.ones((l, l), dtype=bool))
    return jnp.exp(jnp.where(mask, diff, -jnp.inf))


def ssd_mixer(xs, z, Bm, Cm, dt_raw, conv_w, conv_b, dt_bias, a_log, d_skip, norm_w):
    Bsz, T = xs.shape[:2]
    xbc = jax.nn.silu(causal_dwconv(jnp.concatenate([xs, Bm, Cm], axis=-1), conv_w, conv_b))
    xs = xbc[..., :D_INNER]
    Bm = xbc[..., D_INNER:D_INNER + W_BC]
    Cm = xbc[..., D_INNER + W_BC:]
    dt = jax.nn.softplus(dt_raw.astype(jnp.float32) + dt_bias.astype(jnp.float32))
    A = -jnp.exp(a_log.astype(jnp.float32))
    lead = CHUNK - N_META
    n_real = T - N_META
    tail = (-(-n_real // CHUNK)) * CHUNK - n_real
    Tp = lead + T + tail
    nc = Tp // CHUNK
    hpg = SSM_HEADS // SSM_GROUPS

    def pad_t(a):
        return jnp.pad(a, [(0, 0), (lead, tail)] + [(0, 0)] * (a.ndim - 2))

    X = pad_t(xs).astype(jnp.float32).reshape(Bsz, nc, CHUNK, SSM_GROUPS, hpg, SSM_HEAD_DIM)
    dtp = pad_t(dt).reshape(Bsz, nc, CHUNK, SSM_GROUPS, hpg)
    Bc = pad_t(Bm).astype(jnp.float32).reshape(Bsz, nc, CHUNK, SSM_GROUPS, D_STATE)
    Cc = pad_t(Cm).astype(jnp.float32).reshape(Bsz, nc, CHUNK, SSM_GROUPS, D_STATE)
    a = jnp.moveaxis(dtp * A.reshape(SSM_GROUPS, hpg), 2, -1)
    a_cs = jnp.cumsum(a, axis=-1)
    Xdt = X * dtp[..., None]
    Lmat = segsum_exp(a_cs)
    cb = jnp.einsum('bclgn,bcsgn->bcgls', Cc, Bc)
    y_diag = jnp.einsum('bcgls,bcgjls,bcsgjp->bclgjp', cb, Lmat, Xdt)
    decay_states = jnp.exp(a_cs[..., -1:] - a_cs)
    states = jnp.einsum('bclgn,bcgjl,bclgjp->bcgjpn', Bc, decay_states, Xdt)
    chunk_decay = jnp.exp(a_cs[..., -1])

    def step(hc, inp):
        dec, st = inp
        return hc * dec[..., None, None] + st, hc

    h0 = jnp.zeros((Bsz, SSM_GROUPS, hpg, SSM_HEAD_DIM, D_STATE), jnp.float32)
    _, prev = lax.scan(step, h0, (jnp.moveaxis(chunk_decay, 1, 0), jnp.moveaxis(states, 1, 0)))
    prev = jnp.moveaxis(prev, 0, 1)
    y_off = jnp.einsum('bclgn,bcgjpn,bcgjl->bclgjp', Cc, prev, jnp.exp(a_cs))
    y = y_diag + y_off + X * d_skip.astype(jnp.float32).reshape(SSM_GROUPS, hpg)[..., None]
    y = y.reshape(Bsz, Tp, D_INNER)[:, lead:lead + T]
    y = y * jax.nn.silu(z.astype(jnp.float32))
    yg = y.reshape(Bsz, T, SSM_GROUPS, D_INNER // SSM_GROUPS)
    yg = yg * lax.rsqrt(jnp.mean(yg * yg, axis=-1, keepdims=True) + NORM_EPS)
    y = yg.reshape(Bsz, T, D_INNER) * norm_w.astype(jnp.float32)
    return y.astype(xs.dtype)


def peer(h, w_query, sub_keys, u, v):
    Bsz, T, D = h.shape
    n = Bsz * T
    n_blk = -(-n // PEER_TOK_BLOCK)
    hf = jnp.pad(h.reshape(n, D), [(0, n_blk * PEER_TOK_BLOCK - n), (0, 0)])
    blocks = hf.reshape(n_blk, PEER_TOK_BLOCK, D)
    half = PEER_KEY_DIM // 2
    keys_f = sub_keys.astype(jnp.float32)

    def one_block(xb):
        q = (xb @ w_query).astype(jnp.float32).reshape(PEER_TOK_BLOCK, PEER_HEADS, 2, half)
        s = jnp.einsum('thcd,ckd->thck', q, keys_f)
        top_s, top_i = lax.top_k(s, PEER_TOPK)
        cand_s = (top_s[:, :, 0, :, None] + top_s[:, :, 1, None, :]).reshape(PEER_TOK_BLOCK, PEER_HEADS, -1)
        cand_i = (top_i[:, :, 0, :, None] * N_KEYS + top_i[:, :, 1, None, :]).reshape(PEER_TOK_BLOCK, PEER_HEADS, -1)
        best_s, pos = lax.top_k(cand_s, PEER_TOPK)
        expert = jnp.take_along_axis(cand_i, pos, axis=-1)
        g = jax.nn.softmax(best_s, axis=-1)
        u_sel = u[expert].astype(jnp.float32)
        v_sel = v[expert].astype(jnp.float32)
        act = jax.nn.gelu(jnp.einsum('td,thkd->thk', xb.astype(jnp.float32), u_sel), approximate=False)
        out = jnp.einsum('thk,thkd->td', g * act, v_sel)
        return out.astype(h.dtype)

    out = lax.map(one_block, blocks).reshape(n_blk * PEER_TOK_BLOCK, D)[:n]
    return out.reshape(Bsz, T, D)


def setup_inputs(seed: int = 0) -> dict:
    key = jax.random.key(seed)
    ks = jax.random.split(key, 20)
    nrm = jax.random.normal
    dt0 = jnp.exp(jax.random.uniform(ks[6], (DEPTH, SSM_HEADS)) * (math.log(0.1) - math.log(0.001)) + math.log(0.001))
    return {
        'x': nrm(ks[0], (BATCH, SEQ, D_MODEL), jnp.float32),
        'meta_tokens': nrm(ks[1], (N_META, D_MODEL), jnp.float32),
        'norm_mix_w': 1.0 + 0.02 * nrm(ks[2], (DEPTH, D_MODEL), jnp.float32),
        'w_in': nrm(ks[3], (DEPTH, D_MODEL, IN_WIDTH), jnp.float32) * D_MODEL ** -0.5,
        'conv_w': nrm(ks[4], (DEPTH, CONV_WIDTH, CONV_DIM), jnp.float32) * CONV_WIDTH ** -0.5,
        'conv_b': 0.02 * nrm(ks[5], (DEPTH, CONV_DIM), jnp.float32),
        'dt_bias': dt0 + jnp.log(-jnp.expm1(-dt0)),
        'a_log': jnp.log(jax.random.uniform(ks[7], (DEPTH, SSM_HEADS), jnp.float32, 1.0, 16.0)),
        'd_skip': 1.0 + 0.1 * nrm(ks[8], (DEPTH, SSM_HEADS), jnp.float32),
        'ssm_norm_w': 1.0 + 0.02 * nrm(ks[9], (DEPTH, D_INNER), jnp.float32),
        'w_branch_attn': nrm(ks[10], (DEPTH, W_Q, D_MODEL), jnp.float32) * W_Q ** -0.5,
        'w_branch_ssm': nrm(ks[11], (DEPTH, D_INNER, D_MODEL), jnp.float32) * D_INNER ** -0.5,
        'w_out': nrm(ks[12], (DEPTH, D_MODEL, D_MODEL), jnp.float32) * D_MODEL ** -0.5,
        'norm_ffn_w': 1.0 + 0.02 * nrm(ks[13], (DEPTH, D_MODEL), jnp.float32),
        'peer_w_query': nrm(ks[14], (DEPTH, D_MODEL, PEER_HEADS * PEER_KEY_DIM), jnp.float32) * D_MODEL ** -0.5,
        'peer_sub_keys': nrm(ks[15], (DEPTH, 2, N_KEYS, PEER_KEY_DIM // 2), jnp.float32) * (PEER_KEY_DIM // 2) ** -0.5,
        'peer_u': nrm(ks[16], (DEPTH, N_EXPERTS, D_MODEL), jnp.float32) * D_MODEL ** -0.5,
        'peer_v': nrm(ks[17], (DEPTH, N_EXPERTS, D_MODEL), jnp.float32) * PEER_HEADS ** -0.5,
        'norm_final_w': 1.0 + 0.02 * nrm(ks[18], (D_MODEL,), jnp.float32),
    }


def reference(x, meta_tokens, norm_mix_w, w_in, conv_w, conv_b, dt_bias, a_log, d_skip, ssm_norm_w,
              w_branch_attn, w_branch_ssm, w_out, norm_ffn_w, peer_w_query, peer_sub_keys, peer_u, peer_v,
              norm_final_w):
    Bsz, S, _ = x.shape
    topk = min(TOPK_MAX, S // 4)
    meta = jnp.broadcast_to(meta_tokens.astype(x.dtype)[None], (Bsz, N_META, D_MODEL))
    h = jnp.concatenate([meta, x], axis=1)
    T = S + N_META
    pos = jnp.arange(T)
    widths = [W_Q, W_KV, W_KV, W_IQ, W_IK, W_IW, D_INNER, D_INNER, W_BC, W_BC, SSM_HEADS]
    split_at = []
    acc = 0
    for wd in widths:
        acc += wd
        split_at.append(acc)
    for l in range(DEPTH):
        hn = rms_norm(h, norm_mix_w[l])
        proj = hn @ w_in[l]
        q, k, v, qi, ki, wi, z, xs, Bm, Cm, dt_raw, gates = jnp.split(proj, split_at, axis=-1)
        q = rope(q.reshape(Bsz, T, ATTN_HEADS, HEAD_DIM), pos)
        k = rope(k.reshape(Bsz, T, ATTN_KV_HEADS, HEAD_DIM), pos)
        v = v.reshape(Bsz, T, ATTN_KV_HEADS, HEAD_DIM)
        qi = rope(qi.reshape(Bsz, T, IDX_HEADS, IDX_DIM), pos)
        ki = rope(ki[:, :, None, :], pos)[:, :, 0]
        attn = dsa_attention(q, k, v, qi, ki, wi, topk)
        ssm = ssd_mixer(xs, z, Bm, Cm, dt_raw, conv_w[l], conv_b[l], dt_bias[l], a_log[l], d_skip[l], ssm_norm_w[l])
        g = jax.nn.sigmoid(gates.astype(jnp.float32)).astype(h.dtype)
        g_attn = g[..., :D_MODEL]
        g_ssm = g[..., D_MODEL:]
        merged = g_attn * (attn @ w_branch_attn[l]) + g_ssm * (ssm @ w_branch_ssm[l])
        h = h + merged @ w_out[l]
        h = h + peer(rms_norm(h, norm_ffn_w[l]), peer_w_query[l], peer_sub_keys[l], peer_u[l], peer_v[l])
    h = rms_norm(h, norm_final_w)
    return h[:, N_META:]
```

```python
import functools
import math
from typing import NamedTuple

import jax
import jax.numpy as jnp
from jax import lax
from jax.experimental import pallas as pl
from jax.experimental.pallas import tpu as pltpu

F32 = jnp.float32
BF16 = jnp.bfloat16
LANES = 128
NEG_BIG = -1e30
VMEM_LIMIT = 56 * 1024 * 1024


class Cfg(NamedTuple):
    d_model: int = 2048
    n_meta: int = 16
    attn_heads: int = 16
    kv_heads: int = 4
    head_dim: int = 128
    idx_heads: int = 16
    idx_dim: int = 64
    topk_max: int = 256
    rope_theta: float = 10000.0
    d_inner: int = 4096
    ssm_head_dim: int = 64
    ssm_groups: int = 8
    d_state: int = 128
    conv_width: int = 4
    chunk: int = 256
    peer_heads: int = 8
    n_keys: int = 128
    peer_key_dim: int = 256
    peer_topk: int = 16
    eps: float = 1e-6

    @property
    def lead(self):
        return self.chunk - self.n_meta

    @property
    def qw(self):
        return self.attn_heads * self.head_dim

    @property
    def kvw(self):
        return self.kv_heads * self.head_dim

    @property
    def iw(self):
        return self.idx_heads * self.idx_dim

    @property
    def ssm_heads(self):
        return self.d_inner // self.ssm_head_dim

    @property
    def hpg(self):
        return self.ssm_heads // self.ssm_groups

    @property
    def gw(self):
        return self.d_inner // self.ssm_groups

    @property
    def bcw(self):
        return self.ssm_groups * self.d_state


TAIL_W = 256
TAIL_DT = 128


def _params(*sem):
    return pltpu.CompilerParams(dimension_semantics=sem, vmem_limit_bytes=VMEM_LIMIT)


def _inproj_kernel(eps, h_ref, nw_ref, w_ref, wt_ref, main_ref, tail_ref, hn_ref):
    @pl.when(pl.program_id(1) == 0)
    def _():
        x = h_ref[...]
        ms = jnp.mean(x * x, axis=-1, keepdims=True)
        hn = (x * lax.rsqrt(ms + eps) * nw_ref[...]).astype(BF16)
        hn_ref[...] = hn
        tail_ref[...] = jnp.dot(hn, wt_ref[...], preferred_element_type=F32)

    main_ref[...] = jnp.dot(hn_ref[...], w_ref[...], preferred_element_type=F32).astype(main_ref.dtype)


def _pick(n, cands):
    for c in cands:
        if n % c == 0:
            return c
    raise ValueError(f"no tile for {n}")


def _inproj(cfg, h2d, norm_w, w_main, w_tail):
    m, d = h2d.shape
    nm = w_main.shape[1]
    tm = _pick(m, (512, 256))
    tn = _pick(nm, (1024, 512, 256))
    return pl.pallas_call(
        functools.partial(_inproj_kernel, cfg.eps),
        grid=(m // tm, nm // tn),
        in_specs=[
            pl.BlockSpec((tm, d), lambda i, j: (i, 0)),
            pl.BlockSpec((1, d), lambda i, j: (0, 0)),
            pl.BlockSpec((d, tn), lambda i, j: (0, j)),
            pl.BlockSpec((d, TAIL_W), lambda i, j: (0, 0)),
        ],
        out_specs=[
            pl.BlockSpec((tm, tn), lambda i, j: (i, j)),
            pl.BlockSpec((tm, TAIL_W), lambda i, j: (i, 0)),
        ],
        out_shape=[
            jax.ShapeDtypeStruct((m, nm), BF16),
            jax.ShapeDtypeStruct((m, TAIL_W), F32),
        ],
        scratch_shapes=[pltpu.VMEM((tm, d), BF16)],
        compiler_params=_params("parallel", "arbitrary"),
        name="inproj",
    )(h2d, norm_w.reshape(1, d), w_main, w_tail)


def _rope_kernel(n_q, n_qk, n_ipair, idx_dim, q_scale, qk_ref, qi_ref, tail_ref, cs_ref, sn_ref, csi_ref, sni_ref,
                 qk_out, qi_out, ki_out):
    cs = cs_ref[...]
    sn = sn_ref[...]
    for hd in range(n_qk):
        sl = slice(hd * LANES, (hd + 1) * LANES)
        x = qk_ref[:, sl].astype(F32)
        y = x * cs + pltpu.roll(x, LANES // 2, 1) * sn
        if hd < n_q:
            y = y * q_scale
        qk_out[:, sl] = y.astype(BF16)
    csi = csi_ref[...]
    sni = sni_ref[...]
    half = idx_dim // 2
    lane = lax.broadcasted_iota(jnp.int32, csi.shape, 1)
    first = (lane % idx_dim) < half

    def rope_idx(x):
        rot = jnp.where(first, pltpu.roll(x, LANES - half, 1), pltpu.roll(x, half, 1))
        return x * csi + rot * sni

    for p in range(n_ipair):
        sl = slice(p * LANES, (p + 1) * LANES)
        qi_out[:, sl] = rope_idx(qi_ref[:, sl].astype(F32)).astype(BF16)
    kt = rope_idx(tail_ref[:, 0:LANES])
    klo = jnp.where(lane < idx_dim, kt, 0.0)
    ki_out[:, 0:LANES] = klo.astype(BF16)
    ki_out[:, LANES:2 * LANES] = pltpu.roll(klo, idx_dim, 1).astype(BF16)


def _rope(cfg, main, tail, tabs, n_rows_per_batch):
    m = main.shape[0]
    tm = 256
    nb = n_rows_per_batch // tm
    n_q, n_kv = cfg.attn_heads, cfg.kv_heads
    qkw = cfg.qw + cfg.kvw
    assert cfg.head_dim == LANES and 2 * cfg.idx_dim == LANES
    qi_off = cfg.qw + 2 * cfg.kvw
    assert qi_off % cfg.iw == 0
    cs, sn, csi, sni = tabs
    tab_spec = pl.BlockSpec((tm, LANES), lambda i: (i % nb, 0))
    return pl.pallas_call(
        functools.partial(_rope_kernel, n_q, n_q + n_kv, cfg.iw // LANES, cfg.idx_dim, cfg.head_dim ** -0.5),
        grid=(m // tm,),
        in_specs=[
            pl.BlockSpec((tm, qkw), lambda i: (i, 0)),
            pl.BlockSpec((tm, cfg.iw), lambda i: (i, qi_off // cfg.iw)),
            pl.BlockSpec((tm, TAIL_W), lambda i: (i, 0)),
            tab_spec, tab_spec, tab_spec, tab_spec,
        ],
        out_specs=[
            pl.BlockSpec((tm, qkw), lambda i: (i, 0)),
            pl.BlockSpec((tm, cfg.iw), lambda i: (i, 0)),
            pl.BlockSpec((tm, 2 * LANES), lambda i: (i, 0)),
        ],
        out_shape=[
            jax.ShapeDtypeStruct((m, qkw), BF16),
            jax.ShapeDtypeStruct((m, cfg.iw), BF16),
            jax.ShapeDtypeStruct((m, 2 * LANES), BF16),
        ],
        compiler_params=_params("parallel"),
        name="rope",
    )(main, main, tail, cs, sn, csi, sni)


def _rope_tables(cfg, n_rows):
    pos = (jnp.arange(n_rows) - cfg.lead).astype(F32)

    def tab(dim):
        half = dim // 2
        inv = cfg.rope_theta ** (-jnp.arange(half, dtype=F32) / half)
        ang = pos[:, None] * inv[None, :]
        c, s = jnp.cos(ang), jnp.sin(ang)
        reps = LANES // dim
        return jnp.tile(jnp.concatenate([c, c], -1), (1, reps)), jnp.tile(jnp.concatenate([-s, s], -1), (1, reps))

    cs, sn = tab(cfg.head_dim)
    csi, sni = tab(cfg.idx_dim)
    return cs, sn, csi, sni


def _dsa_kernel(cfg, topk, qi_ref, tail_ref, ki_ref, q_ref, k_ref, v_ref, o_ref, key_ref, qs_ref, acc_ref, m_ref, l_ref):
    ch = cfg.chunk
    c = pl.program_id(1) + 1
    n_kc = c + 1
    rep = cfg.attn_heads // cfg.kv_heads
    hd = cfg.head_dim
    row0 = c * ch
    int_min = jnp.int32(-2 ** 31)

    w_all = tail_ref[:, cfg.idx_dim:cfg.idx_dim + cfg.idx_heads] * (cfg.idx_heads ** -0.5 * cfg.idx_dim ** -0.5)
    q_row = row0 + lax.broadcasted_iota(jnp.int32, (ch, ch), 0)
    k_col = lax.broadcasted_iota(jnp.int32, (ch, ch), 1)

    def score_chunk(kc, carry):
        koff = pl.multiple_of(kc * ch, ch)
        k_lo = ki_ref[pl.ds(koff, ch), 0:LANES]
        k_hi = ki_ref[pl.ds(koff, ch), LANES:2 * LANES]
        acc = jnp.zeros((ch, ch), F32)
        for p in range(cfg.idx_heads // 2):
            qp = qi_ref[:, p * LANES:(p + 1) * LANES]
            for half, kk in ((0, k_lo), (1, k_hi)):
                h = 2 * p + half
                lg = lax.dot_general(qp, kk, (((1,), (1,)), ((), ())), preferred_element_type=F32)
                acc = acc + jnp.maximum(lg, 0.0) * w_all[:, h:h + 1]
        k_row = koff + k_col
        adm = (k_row >= cfg.lead) & (k_row <= q_row)
        sc = jnp.where(adm, acc, -jnp.inf)
        bits = lax.bitcast_convert_type(sc, jnp.int32)
        key_ref[:, pl.ds(koff, ch)] = bits ^ ((bits >> 31) & jnp.int32(0x7FFFFFFF))
        return carry

    lax.fori_loop(0, n_kc, score_chunk, 0)

    rg = 64
    for g in range(ch // rg):
        rows = slice(g * rg, (g + 1) * rg)

        def bit_step(it, tau_u):
            cand_u = tau_u | lax.shift_left(jnp.int32(1), jnp.int32(31) - it)
            cand = jnp.broadcast_to(cand_u ^ int_min, (rg, LANES))

            def cnt_chunk(kc, cnt):
                koff = pl.multiple_of(kc * ch, ch)
                for s in range(ch // LANES):
                    kv = key_ref[rows, pl.ds(koff + s * LANES, LANES)]
                    cnt = cnt + jnp.where(kv >= cand, 1.0, 0.0)
                return cnt

            cnt = lax.fori_loop(0, n_kc, cnt_chunk, jnp.zeros((rg, LANES), F32))
            tot = jnp.sum(cnt, axis=1, keepdims=True)
            return jnp.where(tot >= float(topk), cand_u, tau_u)

        tau_u = lax.fori_loop(0, 32, bit_step, jnp.zeros((rg, 1), jnp.int32))
        tau = jnp.broadcast_to(tau_u ^ int_min, (rg, LANES))

        def bias_chunk(kc, carry):
            koff = pl.multiple_of(kc * ch, ch)
            for s in range(ch // LANES):
                col = pl.ds(koff + s * LANES, LANES)
                kv = key_ref[rows, col]
                k_row = koff + s * LANES + lax.broadcasted_iota(jnp.int32, (rg, LANES), 1)
                qr = row0 + g * rg + lax.broadcasted_iota(jnp.int32, (rg, LANES), 0)
                sel = (kv >= tau) & (k_row >= cfg.lead) & (k_row <= qr)
                key_ref[rows, col] = lax.bitcast_convert_type(jnp.where(sel, 0.0, NEG_BIG).astype(F32), jnp.int32)
            return carry

        lax.fori_loop(0, n_kc, bias_chunk, 0)

    for g in range(cfg.kv_heads):
        for r in range(rep):
            h = g * rep + r
            qs_ref[g, r * ch:(r + 1) * ch, :] = q_ref[:, h * hd:(h + 1) * hd]
    acc_ref[...] = jnp.zeros(acc_ref.shape, F32)
    m_ref[...] = jnp.full(m_ref.shape, NEG_BIG, F32)
    l_ref[...] = jnp.zeros(l_ref.shape, F32)

    def attn_chunk(kc, carry):
        koff = pl.multiple_of(kc * ch, ch)
        bias = lax.bitcast_convert_type(key_ref[:, pl.ds(koff, ch)], F32)
        bias = jnp.concatenate([bias] * rep, axis=0)
        for g in range(cfg.kv_heads):
            kg = k_ref[pl.ds(koff, ch), g * hd:(g + 1) * hd]
            vg = v_ref[pl.ds(koff, ch), g * hd:(g + 1) * hd]
            s = lax.dot_general(qs_ref[g], kg, (((1,), (1,)), ((), ())), preferred_element_type=F32) + bias
            m_old = m_ref[g]
            m_new = jnp.maximum(m_old, jnp.max(s, axis=1, keepdims=True))
            alpha = jnp.exp(m_old - m_new)
            p = jnp.exp(s - m_new[:, 0:1])
            l_ref[g] = alpha * l_ref[g] + jnp.sum(p, axis=1, keepdims=True)
            acc_ref[g] = alpha * acc_ref[g] + jnp.dot(p.astype(BF16), vg, preferred_element_type=F32)
            m_ref[g] = m_new
        return carry

    lax.fori_loop(0, n_kc, attn_chunk, 0)

    for g in range(cfg.kv_heads):
        og = acc_ref[g] / l_ref[g]
        for r in range(rep):
            h = g * rep + r
            o_ref[:, h * hd:(h + 1) * hd] = og[r * ch:(r + 1) * ch, :].astype(o_ref.dtype)


def _dsa(cfg, topk, qk_r, qi_r, ki2, main, tail, bsz, n_rows):
    ch = cfg.chunk
    nq = n_rows // ch - 1
    rep = cfg.attn_heads // cfg.kv_heads
    qk3 = qk_r.reshape(bsz, n_rows, -1)
    qi3 = qi_r.reshape(bsz, n_rows, -1)
    ki3 = ki2.reshape(bsz, n_rows, -1)
    main3 = main.reshape(bsz, n_rows, -1)
    tail3 = tail.reshape(bsz, n_rows, -1)
    assert cfg.qw % cfg.kvw == 0
    k_blk = cfg.qw // cfg.kvw
    return pl.pallas_call(
        functools.partial(_dsa_kernel, cfg, topk),
        grid=(bsz, nq),
        in_specs=[
            pl.BlockSpec((None, ch, cfg.iw), lambda b, i: (b, i + 1, 0)),
            pl.BlockSpec((None, ch, TAIL_W), lambda b, i: (b, i + 1, 0)),
            pl.BlockSpec((None, n_rows, 2 * LANES), lambda b, i: (b, 0, 0)),
            pl.BlockSpec((None, ch, cfg.qw), lambda b, i: (b, i + 1, 0)),
            pl.BlockSpec((None, n_rows, cfg.kvw), lambda b, i: (b, 0, k_blk)),
            pl.BlockSpec((None, n_rows, cfg.kvw), lambda b, i: (b, 0, k_blk + 1)),
        ],
        out_specs=pl.BlockSpec((None, ch, cfg.qw), lambda b, i: (b, i, 0)),
        out_shape=jax.ShapeDtypeStruct((bsz, nq * ch, cfg.qw), BF16),
        scratch_shapes=[
            pltpu.VMEM((ch, n_rows), jnp.int32),
            pltpu.VMEM((cfg.kv_heads, rep * ch, cfg.head_dim), BF16),
            pltpu.VMEM((cfg.kv_heads, rep * ch, cfg.head_dim), F32),
            pltpu.VMEM((cfg.kv_heads, rep * ch, 1), F32),
            pltpu.VMEM((cfg.kv_heads, rep * ch, 1), F32),
        ],
        compiler_params=_params("parallel", "arbitrary"),
        name="dsa",
    )(qi3, tail3, ki3, qk3, qk3, main3)


def _split3(a):
    hi = a.astype(BF16)
    r1 = a - hi.astype(F32)
    mid = r1.astype(BF16)
    lo = (r1 - mid.astype(F32)).astype(BF16)
    return hi, mid, lo


def _dot3_rhs(ones_lhs, a):
    return sum(jnp.dot(ones_lhs, t, preferred_element_type=F32) for t in _split3(a))


def _dot3_lhs(a, ones_rhs):
    return sum(jnp.dot(t, ones_rhs, preferred_element_type=F32) for t in _split3(a))


def _silu(x):
    return x * jax.nn.sigmoid(x)


def _ssd_kernel(cfg, z_ref, x_ref, b_ref, c_ref, dtc_ref, dtr_ref, cwx_ref, cwb_ref, cwc_ref, cbx_ref, cbb_ref, cbc_ref,
                dbc_ref, dbr_ref, alc_ref, alr_ref, dsk_ref, nw_ref, y_ref, state_ref, carry_ref, ext_ref):
    ch, gw, ns, hdim, hpg = cfg.chunk, cfg.gw, cfg.d_state, cfg.ssm_head_dim, cfg.hpg
    cwid = cfg.conv_width
    c = pl.program_id(1)
    g = pl.program_id(2)
    ew = gw + 2 * ns

    @pl.when(c == 0)
    def _():
        state_ref[g] = jnp.zeros((ns, gw), F32)
        carry_ref[g] = jnp.zeros((8, ew), F32)

    ext_ref[0:8, :] = carry_ref[g]
    ext_ref[8:8 + ch, 0:gw] = x_ref[...].astype(F32)
    ext_ref[8:8 + ch, gw:gw + ns] = b_ref[...].astype(F32)
    ext_ref[8:8 + ch, gw + ns:ew] = c_ref[...].astype(F32)
    carry_ref[g] = ext_ref[ch:ch + 8, :]

    def conv(lo, hi, cw_ref, cb_ref):
        acc = jnp.zeros((ch, hi - lo), F32) + cb_ref[...]
        for w in range(cwid):
            acc = acc + ext_ref[8 - (cwid - 1) + w:8 - (cwid - 1) + w + ch, lo:hi] * cw_ref[w:w + 1, :]
        return _silu(acc)

    xc = conv(0, gw, cwx_ref, cbx_ref)
    bc = conv(gw, gw + ns, cwb_ref, cbb_ref)
    cc = conv(gw + ns, ew, cwc_ref, cbc_ref)

    rowi = lax.broadcasted_iota(jnp.int32, (ch, hpg), 0)
    coli = lax.broadcasted_iota(jnp.int32, (hpg, ch), 1)
    is_pad_c = (rowi < cfg.lead) & (c == 0)
    is_pad_r = (coli < cfg.lead) & (c == 0)
    dt_c = jnp.where(is_pad_c, 0.0, jax.nn.softplus(dtc_ref[...] + dbc_ref[...]))
    dt_r = jnp.where(is_pad_r, 0.0, jax.nn.softplus(dtr_ref[...] + dbr_ref[...]))
    a_c = dt_c * (-jnp.exp(alc_ref[...]))
    a_r = dt_r * (-jnp.exp(alr_ref[...]))
    li = lax.broadcasted_iota(jnp.int32, (ch, ch), 0)
    si = lax.broadcasted_iota(jnp.int32, (ch, ch), 1)
    causal = li >= si
    tri = jnp.where(causal, 1.0, 0.0).astype(BF16)
    tri_t = jnp.where(li <= si, 1.0, 0.0).astype(BF16)
    acs_c = _dot3_rhs(tri, a_c)
    acs_r = _dot3_lhs(a_r, tri_t)
    a_end = acs_r[:, ch - 1:ch]
    wdec_r = dt_r * jnp.exp(a_end - acs_r)

    cb = lax.dot_general(cc.astype(BF16), bc.astype(BF16), (((1,), (1,)), ((), ())), preferred_element_type=F32)
    bt = bc.T
    xb = xc.astype(BF16)
    prev = state_ref[g]
    prev_b = prev.astype(BF16)
    lane_p = lax.broadcasted_iota(jnp.int32, (ch, 2 * hdim), 1)
    lane_s = lax.broadcasted_iota(jnp.int32, (ns, 2 * hdim), 1)

    y_pairs = []
    st_pairs = []
    for pr in range(hpg // 2):
        psl = slice(pr * 2 * hdim, (pr + 1) * 2 * hdim)
        x_pair = xb[:, psl]
        prev_pair = prev_b[:, psl]
        ys, sts = [], []
        for half in range(2):
            j = 2 * pr + half
            col = acs_c[:, j:j + 1]
            row = acs_r[j:j + 1, :]
            lmat = jnp.where(causal, jnp.exp(jnp.where(causal, col - row, 0.0)), 0.0)
            mmat = (cb * lmat * dt_r[j:j + 1, :]).astype(BF16)
            ce = (cc * jnp.exp(col)).astype(BF16)
            ys.append(jnp.dot(mmat, x_pair, preferred_element_type=F32)
                      + jnp.dot(ce, prev_pair, preferred_element_type=F32))
            btw = (bt * wdec_r[j:j + 1, :]).astype(BF16)
            sts.append(jnp.dot(btw, x_pair, preferred_element_type=F32))
        y_pairs.append(jnp.where(lane_p < hdim, ys[0], ys[1]))
        st_pairs.append(jnp.where(lane_s < hdim, sts[0], sts[1]))
    y = jnp.concatenate(y_pairs, axis=1)
    st = jnp.concatenate(st_pairs, axis=1)

    lane_g = lax.broadcasted_iota(jnp.int32, (1, gw), 1) // hdim
    cdec = jnp.zeros((1, gw), F32)
    for j in range(hpg):
        cdec = jnp.where(lane_g == j, jnp.exp(a_end[j:j + 1, :]), cdec)
    state_ref[g] = prev * cdec + st

    y = y + xc * dsk_ref[...]
    y = y * _silu(z_ref[...].astype(F32))
    ms = jnp.mean(y * y, axis=-1, keepdims=True)
    y_ref[...] = (y * lax.rsqrt(ms + cfg.eps) * nw_ref[...]).astype(y_ref.dtype)


def _ssd(cfg, main, dt_raw, conv_w, conv_b, dt_bias, a_log, d_skip, norm_w, bsz, n_rows):
    ch, gw, ns, hpg, ng = cfg.chunk, cfg.gw, cfg.d_state, cfg.hpg, cfg.ssm_groups
    nc = n_rows // ch
    di = cfg.d_inner
    main3 = main.reshape(bsz, n_rows, -1)
    z_off = cfg.qw + 2 * cfg.kvw + cfg.iw
    x_off = z_off + di
    b_off = x_off + di
    c_off = b_off + cfg.bcw
    assert z_off % gw == 0 and b_off % ns == 0 and hpg % 2 == 0 and 2 * cfg.ssm_head_dim == LANES
    dtc = dt_raw.reshape(bsz, n_rows, ng, hpg).transpose(0, 2, 1, 3)
    dtr = dt_raw.transpose(0, 2, 1)
    cwx, cwb, cwc = conv_w[:, :di], conv_w[:, di:di + cfg.bcw], conv_w[:, di + cfg.bcw:]
    cb2 = conv_b.reshape(1, -1)
    cbx, cbb, cbc = cb2[:, :di], cb2[:, di:di + cfg.bcw], cb2[:, di + cfg.bcw:]
    ew = gw + 2 * ns
    cw = cfg.conv_width
    grp = lambda b, c, g: (b, c, g)
    return pl.pallas_call(
        functools.partial(_ssd_kernel, cfg),
        grid=(bsz, nc, ng),
        in_specs=[
            pl.BlockSpec((None, ch, gw), lambda b, c, g: (b, c, z_off // gw + g)),
            pl.BlockSpec((None, ch, gw), lambda b, c, g: (b, c, x_off // gw + g)),
            pl.BlockSpec((None, ch, ns), lambda b, c, g: (b, c, b_off // ns + g)),
            pl.BlockSpec((None, ch, ns), lambda b, c, g: (b, c, c_off // ns + g)),
            pl.BlockSpec((None, None, ch, hpg), lambda b, c, g: (b, g, c, 0)),
            pl.BlockSpec((None, hpg, ch), lambda b, c, g: (b, g, c)),
            pl.BlockSpec((cw, gw), lambda b, c, g: (0, g)),
            pl.BlockSpec((cw, ns), lambda b, c, g: (0, g)),
            pl.BlockSpec((cw, ns), lambda b, c, g: (0, g)),
            pl.BlockSpec((1, gw), lambda b, c, g: (0, g)),
            pl.BlockSpec((1, ns), lambda b, c, g: (0, g)),
            pl.BlockSpec((1, ns), lambda b, c, g: (0, g)),
            pl.BlockSpec((None, 1, hpg), lambda b, c, g: (g, 0, 0)),
            pl.BlockSpec((None, hpg, 1), lambda b, c, g: (g, 0, 0)),
            pl.BlockSpec((None, 1, hpg), lambda b, c, g: (g, 0, 0)),
            pl.BlockSpec((None, hpg, 1), lambda b, c, g: (g, 0, 0)),
            pl.BlockSpec((1, gw), lambda b, c, g: (0, g)),
            pl.BlockSpec((1, gw), lambda b, c, g: (0, g)),
        ],
        out_specs=pl.BlockSpec((None, ch, gw), grp),
        out_shape=jax.ShapeDtypeStruct((bsz, n_rows, di), BF16),
        scratch_shapes=[
            pltpu.VMEM((ng, ns, gw), F32),
            pltpu.VMEM((ng, 8, ew), F32),
            pltpu.VMEM((ch + 8, ew), F32),
        ],
        compiler_params=_params("parallel", "arbitrary", "arbitrary"),
        name="ssd",
    )(main3, main3, main3, main3, dtc, dtr, cwx, cwb, cwc, cbx, cbb, cbc,
      dt_bias.reshape(ng, 1, hpg), dt_bias.reshape(ng, hpg, 1), a_log.reshape(ng, 1, hpg), a_log.reshape(ng, hpg, 1),
      jnp.repeat(d_skip, cfg.ssm_head_dim).reshape(1, di), norm_w.reshape(1, di))


def _merge_kernel(attn_ref, ssm_ref, ga_ref, gs_ref, wa_ref, ws_ref, o_ref):
    pa = jnp.dot(attn_ref[...], wa_ref[...], preferred_element_type=F32)
    ps = jnp.dot(ssm_ref[...], ws_ref[...], preferred_element_type=F32)
    ga = jax.nn.sigmoid(ga_ref[...].astype(F32))
    gs = jax.nn.sigmoid(gs_ref[...].astype(F32))
    o_ref[...] = (ga * pa + gs * ps).astype(o_ref.dtype)


def _merge(cfg, attn, ssm, main, w_a, w_s, bsz, n_rows):
    ch, d = cfg.chunk, cfg.d_model
    nq = n_rows // ch - 1
    tn = _pick(d, (1024, 512, 256))
    main3 = main.reshape(bsz, n_rows, -1)
    g_off = cfg.qw + 2 * cfg.kvw + cfg.iw + 2 * cfg.d_inner + 2 * cfg.bcw
    assert g_off % tn == 0
    return pl.pallas_call(
        _merge_kernel,
        grid=(d // tn, bsz, nq),
        in_specs=[
            pl.BlockSpec((None, ch, cfg.qw), lambda j, b, i: (b, i, 0)),
            pl.BlockSpec((None, ch, cfg.d_inner), lambda j, b, i: (b, i + 1, 0)),
            pl.BlockSpec((None, ch, tn), lambda j, b, i: (b, i + 1, g_off // tn + j)),
            pl.BlockSpec((None, ch, tn), lambda j, b, i: (b, i + 1, (g_off + d) // tn + j)),
            pl.BlockSpec((cfg.qw, tn), lambda j, b, i: (0, j)),
            pl.BlockSpec((cfg.d_inner, tn), lambda j, b, i: (0, j)),
        ],
        out_specs=pl.BlockSpec((None, ch, tn), lambda j, b, i: (b, i, j)),
        out_shape=jax.ShapeDtypeStruct((bsz, nq * ch, d), BF16),
        compiler_params=_params("parallel", "parallel", "parallel"),
        name="merge",
    )(attn, ssm, main3, main3, w_a, w_s)


def _outproj_kernel(eps, m_ref, h_ref, w_ref, nw_ref, h2_ref, hn_ref):
    h2 = h_ref[...] + jnp.dot(m_ref[...], w_ref[...], preferred_element_type=F32)
    h2_ref[...] = h2
    ms = jnp.mean(h2 * h2, axis=-1, keepdims=True)
    hn_ref[...] = (h2 * lax.rsqrt(ms + eps) * nw_ref[...]).astype(hn_ref.dtype)


def _outproj(cfg, merged, h3, w_out, norm_w, bsz, n_rows):
    ch, d = cfg.chunk, cfg.d_model
    nq = n_rows // ch - 1
    return pl.pallas_call(
        functools.partial(_outproj_kernel, cfg.eps),
        grid=(bsz, nq),
        in_specs=[
            pl.BlockSpec((None, ch, d), lambda b, i: (b, i, 0)),
            pl.BlockSpec((None, ch, d), lambda b, i: (b, i + 1, 0)),
            pl.BlockSpec((d, d), lambda b, i: (0, 0)),
            pl.BlockSpec((1, d), lambda b, i: (0, 0)),
        ],
        out_specs=[
            pl.BlockSpec((None, ch, d), lambda b, i: (b, i, 0)),
            pl.BlockSpec((None, ch, d), lambda b, i: (b, i, 0)),
        ],
        out_shape=[
            jax.ShapeDtypeStruct((bsz, nq * ch, d), F32),
            jax.ShapeDtypeStruct((bsz, nq * ch, d), BF16),
        ],
        compiler_params=_params("parallel", "parallel"),
        name="outproj",
    )(merged, h3, w_out, norm_w.reshape(1, d))


def _topk_desc(s, k):
    out = []
    for _ in range(k):
        m = jnp.max(s, axis=0, keepdims=True)
        out.append(m)
        s = jnp.where(s >= m, -jnp.inf, s)
    return out


def _route_kernel(cfg, x_ref, wq_ref, keys_ref, s0_ref, s1_ref, e0_ref, e1_ref, thr_ref):
    nk, k = cfg.n_keys, cfg.peer_topk
    half = cfg.peer_key_dim // 2
    q = jnp.dot(x_ref[...], wq_ref[...], preferred_element_type=F32)
    for h in range(cfg.peer_heads):
        st = []
        for cidx in range(2):
            off = (2 * h + cidx) * half
            qc = q[:, off:off + half].astype(BF16)
            st.append(lax.dot_general(keys_ref[cidx], qc, (((1,), (1,)), ((), ())), preferred_element_type=F32))
        t0 = _topk_desc(st[0], k)
        t1 = _topk_desc(st[1], k)
        t1m = jnp.concatenate(t1, axis=0)
        cand = jnp.concatenate([a + t1m for a in t0], axis=0)
        best = _topk_desc(cand, k)
        mx = best[0]
        z = sum(jnp.exp(b - mx) for b in best)
        s0_ref[h] = st[0]
        s1_ref[h] = st[1]
        e0_ref[h] = jnp.exp(st[0] - t0[0])
        e1_ref[h] = jnp.exp(st[1] - t1[0]) / z
        thr_ref[h] = jnp.broadcast_to(best[k - 1], (8, best[0].shape[1]))


def _route(cfg, hn2, wq, keys):
    n, d = hn2.shape
    tm = 256
    nh, nk = cfg.peer_heads, cfg.n_keys
    qd = nh * cfg.peer_key_dim
    assert nk == LANES and cfg.peer_key_dim // 2 == LANES
    big = pl.BlockSpec((nh, nk, tm), lambda i: (0, 0, i))
    shp = jax.ShapeDtypeStruct((nh, nk, n), F32)
    return pl.pallas_call(
        functools.partial(_route_kernel, cfg),
        grid=(n // tm,),
        in_specs=[
            pl.BlockSpec((tm, d), lambda i: (i, 0)),
            pl.BlockSpec((d, qd), lambda i: (0, 0)),
            pl.BlockSpec((2, nk, cfg.peer_key_dim // 2), lambda i: (0, 0, 0)),
        ],
        out_specs=[big, big, big, big, pl.BlockSpec((nh, 8, tm), lambda i: (0, 0, i))],
        out_shape=[shp, shp, shp, shp, jax.ShapeDtypeStruct((nh, 8, n), F32)],
        compiler_params=_params("parallel"),
        name="peer_route",
    )(hn2, wq, keys)


def _gelu(x):
    return 0.5 * x * (1.0 + lax.erf(x * (2.0 ** -0.5)))


def _peer_kernel(cfg, ic, x_ref, u_ref, vt_ref, s0_ref, s1_ref, e0_ref, e1_ref, thr_ref, o_ref, a_ref, hb_ref):
    nk = cfg.n_keys
    e = pl.program_id(1)

    @pl.when(e == 0)
    def _():
        o_ref[...] = jnp.zeros(o_ref.shape, F32)

    a_ref[...] = lax.dot_general(u_ref[...], x_ref[...], (((1,), (1,)), ((), ())), preferred_element_type=F32)

    def slab(i, carry):
        ioff = pl.multiple_of(i * nk, nk)
        w = jnp.zeros((nk, a_ref.shape[1]), F32)
        for h in range(cfg.peer_heads):
            s0 = s0_ref[h, pl.ds(i, 1), :]
            e0 = e0_ref[h, pl.ds(i, 1), :]
            sel = (s1_ref[h] + s0) >= thr_ref[h, 0:1, :]
            w = w + jnp.where(sel, e1_ref[h] * e0, 0.0)
        hb_ref[pl.ds(ioff, nk), :] = (w * _gelu(a_ref[pl.ds(ioff, nk), :])).astype(BF16)
        return carry

    lax.fori_loop(0, ic, slab, 0)
    o_ref[...] += jnp.dot(vt_ref[...], hb_ref[...], preferred_element_type=F32)


def _peer(cfg, hn2, u_b, vt_b, routes):
    n, d = hn2.shape
    nh, nk = cfg.peer_heads, cfg.n_keys
    tm = _pick(n, (512, 256))
    ic = 8
    ec = ic * nk
    n_e = nk // ic
    s0, s1, e0, e1, thr = routes
    chunked = pl.BlockSpec((nh, ic, tm), lambda t, e: (0, e, t))
    full = pl.BlockSpec((nh, nk, tm), lambda t, e: (0, 0, t))
    return pl.pallas_call(
        functools.partial(_peer_kernel, cfg, ic),
        grid=(n // tm, n_e),
        in_specs=[
            pl.BlockSpec((tm, d), lambda t, e: (t, 0)),
            pl.BlockSpec((ec, d), lambda t, e: (e, 0)),
            pl.BlockSpec((d, ec), lambda t, e: (0, e)),
            chunked, full, chunked, full,
            pl.BlockSpec((nh, 8, tm), lambda t, e: (0, 0, t)),
        ],
        out_specs=pl.BlockSpec((d, tm), lambda t, e: (0, t)),
        out_shape=jax.ShapeDtypeStruct((d, n), F32),
        scratch_shapes=[pltpu.VMEM((ec, tm), F32), pltpu.VMEM((ec, tm), BF16)],
        compiler_params=_params("parallel", "arbitrary"),
        name="peer_experts",
    )(hn2, u_b, vt_b, s0, s1, e0, e1, thr)


def _final_kernel(eps, h2_ref, pt_ref, nw_ref, o_ref):
    h = h2_ref[...] + pt_ref[...].T
    ms = jnp.mean(h * h, axis=-1, keepdims=True)
    o_ref[...] = h * lax.rsqrt(ms + eps) * nw_ref[...]


def _final(cfg, h2, peer_t, norm_w):
    n, d = h2.shape
    tm = 256
    return pl.pallas_call(
        functools.partial(_final_kernel, cfg.eps),
        grid=(n // tm,),
        in_specs=[
            pl.BlockSpec((tm, d), lambda i: (i, 0)),
            pl.BlockSpec((d, tm), lambda i: (0, i)),
            pl.BlockSpec((1, d), lambda i: (0, 0)),
        ],
        out_specs=pl.BlockSpec((tm, d), lambda i: (i, 0)),
        out_shape=jax.ShapeDtypeStruct((n, d), F32),
        compiler_params=_params("parallel"),
        name="final_norm",
    )(h2, peer_t, norm_w.reshape(1, d))


def _layer(cfg, x, meta_tokens, norm_mix_w, w_in, conv_w, conv_b, dt_bias, a_log, d_skip, ssm_norm_w,
           w_branch_attn, w_branch_ssm, w_out, norm_ffn_w, peer_w_query, peer_sub_keys, peer_u, peer_v,
           norm_final_w):
    bsz, seq, d = x.shape
    ch = cfg.chunk
    assert seq % ch == 0 and norm_mix_w.shape[0] == 1
    n_rows = ch + seq
    topk = min(cfg.topk_max, seq // 4)
    h = jnp.concatenate([
        jnp.zeros((bsz, cfg.lead, d), x.dtype),
        jnp.broadcast_to(meta_tokens.astype(x.dtype)[None], (bsz, cfg.n_meta, d)),
        x], axis=1)

    w = w_in[0]
    o_iq = cfg.qw + 2 * cfg.kvw
    o_ik = o_iq + cfg.iw
    o_iw = o_ik + cfg.idx_dim
    o_z = o_iw + cfg.idx_heads
    o_dt = o_z + 2 * cfg.d_inner + 2 * cfg.bcw
    o_g = o_dt + cfg.ssm_heads
    w_main = jnp.concatenate([w[:, :o_ik], w[:, o_z:o_dt], w[:, o_g:]], axis=1).astype(BF16)
    zpad = lambda n: jnp.zeros((d, n), w.dtype)
    w_tail = jnp.concatenate([
        w[:, o_ik:o_z], zpad(TAIL_DT - cfg.idx_dim - cfg.idx_heads),
        w[:, o_dt:o_g], zpad(TAIL_W - TAIL_DT - cfg.ssm_heads)], axis=1).astype(BF16)

    main, tail = _inproj(cfg, h.reshape(bsz * n_rows, d), norm_mix_w[0], w_main, w_tail)
    qk_r, qi_r, ki2 = _rope(cfg, main, tail, _rope_tables(cfg, n_rows), n_rows)
    attn = _dsa(cfg, topk, qk_r, qi_r, ki2, main, tail, bsz, n_rows)
    dt_raw = tail.reshape(bsz, n_rows, TAIL_W)[:, :, TAIL_DT:TAIL_DT + cfg.ssm_heads]
    ssm = _ssd(cfg, main, dt_raw, conv_w[0], conv_b[0], dt_bias[0], a_log[0], d_skip[0], ssm_norm_w[0], bsz, n_rows)
    merged = _merge(cfg, attn, ssm, main, w_branch_attn[0].astype(BF16), w_branch_ssm[0].astype(BF16), bsz, n_rows)
    h2, hn2 = _outproj(cfg, merged, h, w_out[0].astype(BF16), norm_ffn_w[0], bsz, n_rows)
    h2 = h2.reshape(bsz * seq, d)
    hn2 = hn2.reshape(bsz * seq, d)
    routes = _route(cfg, hn2, peer_w_query[0].astype(BF16), peer_sub_keys[0].astype(BF16))
    peer_t = _peer(cfg, hn2, peer_u[0].astype(BF16), peer_v[0].T.astype(BF16), routes)
    out = _final(cfg, h2, peer_t, norm_final_w)
    return out.reshape(bsz, seq, d)


def kernel(x, meta_tokens, norm_mix_w, w_in, conv_w, conv_b, dt_bias, a_log, d_skip, ssm_norm_w, w_branch_attn,
           w_branch_ssm, w_out, norm_ffn_w, peer_w_query, peer_sub_keys, peer_u, peer_v, norm_final_w):
    return _layer(Cfg(), x, meta_tokens, norm_mix_w, w_in, conv_w, conv_b, dt_bias, a_log, d_skip, ssm_norm_w,
                  w_branch_attn, w_branch_ssm, w_out, norm_ffn_w, peer_w_query, peer_sub_keys, peer_u, peer_v,
                  norm_final_w)
```

```python
import functools
import math
from typing import NamedTuple

import jax
import jax.numpy as jnp
from jax import lax
from jax.experimental import pallas as pl
from jax.experimental.pallas import tpu as pltpu

F32 = jnp.float32
BF16 = jnp.bfloat16
LANES = 128
LOG2E = math.log2(math.e)
NEG_BIG = -1e30
VMEM_LIMIT = 56 * 1024 * 1024


class Cfg(NamedTuple):
    d_model: int = 2048
    n_meta: int = 16
    attn_heads: int = 16
    kv_heads: int = 4
    head_dim: int = 128
    idx_heads: int = 16
    idx_dim: int = 64
    topk_max: int = 256
    rope_theta: float = 10000.0
    d_inner: int = 4096
    ssm_head_dim: int = 64
    ssm_groups: int = 8
    d_state: int = 128
    conv_width: int = 4
    chunk: int = 256
    peer_heads: int = 8
    n_keys: int = 128
    peer_key_dim: int = 256
    peer_topk: int = 16
    eps: float = 1e-6

    @property
    def lead(self):
        return self.chunk - self.n_meta

    @property
    def qw(self):
        return self.attn_heads * self.head_dim

    @property
    def kvw(self):
        return self.kv_heads * self.head_dim

    @property
    def iw(self):
        return self.idx_heads * self.idx_dim

    @property
    def ssm_heads(self):
        return self.d_inner // self.ssm_head_dim

    @property
    def hpg(self):
        return self.ssm_heads // self.ssm_groups

    @property
    def gw(self):
        return self.d_inner // self.ssm_groups

    @property
    def bcw(self):
        return self.ssm_groups * self.d_state


TAIL_W = 256
TAIL_DT = 128


def _params(*sem):
    return pltpu.CompilerParams(dimension_semantics=sem, vmem_limit_bytes=VMEM_LIMIT)


def _inproj_kernel(eps, h_ref, nw_ref, w_ref, wt_ref, main_ref, tail_ref, hn_ref):
    @pl.when(pl.program_id(1) == 0)
    def _():
        x = h_ref[...]
        ms = jnp.mean(x * x, axis=-1, keepdims=True)
        hn = (x * lax.rsqrt(ms + eps) * nw_ref[...]).astype(BF16)
        hn_ref[...] = hn
        tail_ref[...] = jnp.dot(hn, wt_ref[...], preferred_element_type=F32)

    main_ref[...] = jnp.dot(hn_ref[...], w_ref[...], preferred_element_type=F32).astype(main_ref.dtype)


def _pick(n, cands):
    for c in cands:
        if n % c == 0:
            return c
    raise ValueError(f"no tile for {n}")


def _inproj(cfg, h2d, norm_w, w_main, w_tail):
    m, d = h2d.shape
    nm = w_main.shape[1]
    tm = _pick(m, (1088, 512, 256))
    tn = _pick(nm, (1024, 512, 256))
    return pl.pallas_call(
        functools.partial(_inproj_kernel, cfg.eps),
        grid=(m // tm, nm // tn),
        in_specs=[
            pl.BlockSpec((tm, d), lambda i, j: (i, 0)),
            pl.BlockSpec((1, d), lambda i, j: (0, 0)),
            pl.BlockSpec((d, tn), lambda i, j: (0, j)),
            pl.BlockSpec((d, TAIL_W), lambda i, j: (0, 0)),
        ],
        out_specs=[
            pl.BlockSpec((tm, tn), lambda i, j: (i, j)),
            pl.BlockSpec((tm, TAIL_W), lambda i, j: (i, 0)),
        ],
        out_shape=[
            jax.ShapeDtypeStruct((m, nm), BF16),
            jax.ShapeDtypeStruct((m, TAIL_W), F32),
        ],
        scratch_shapes=[pltpu.VMEM((tm, d), BF16)],
        compiler_params=_params("parallel", "arbitrary"),
        name="inproj",
    )(h2d, norm_w.reshape(1, d), w_main, w_tail)


def _rope_kernel(n_q, n_qk, n_ipair, idx_dim, q_scale, qk_ref, v_ref, qi_ref, tail_ref, cs_ref, sn_ref, csi_ref,
                 sni_ref, qk_out, qi_out, ki_out, vt_out, tt_out):
    cs = cs_ref[...]
    sn = sn_ref[...]
    for hd in range(n_qk):
        sl = slice(hd * LANES, (hd + 1) * LANES)
        x = qk_ref[:, sl].astype(F32)
        y = x * cs + pltpu.roll(x, LANES // 2, 1) * sn
        if hd < n_q:
            y = y * q_scale
        qk_out[:, sl] = y.astype(BF16)
    csi = csi_ref[...]
    sni = sni_ref[...]
    half = idx_dim // 2
    lane = lax.broadcasted_iota(jnp.int32, csi.shape, 1)
    first = (lane % idx_dim) < half

    def rope_idx(x):
        rot = jnp.where(first, pltpu.roll(x, LANES - half, 1), pltpu.roll(x, half, 1))
        return x * csi + rot * sni

    for p in range(n_ipair):
        sl = slice(p * LANES, (p + 1) * LANES)
        qi_out[:, sl] = rope_idx(qi_ref[:, sl].astype(F32)).astype(BF16)
    kt = rope_idx(tail_ref[:, 0:LANES])
    klo = jnp.where(lane < idx_dim, kt, 0.0)
    ki_out[:, 0:LANES] = klo.astype(BF16)
    ki_out[:, LANES:2 * LANES] = pltpu.roll(klo, idx_dim, 1).astype(BF16)
    vt_out[...] = v_ref[...].astype(F32).T.astype(BF16)
    tt_out[...] = tail_ref[...].T


def _rope(cfg, main, tail, tabs, bsz, n_rows):
    m = main.shape[0]
    tm = 256
    nb = n_rows // tm
    n_q, n_kv = cfg.attn_heads, cfg.kv_heads
    qkw = cfg.qw + cfg.kvw
    assert cfg.head_dim == LANES and 2 * cfg.idx_dim == LANES
    qi_off = cfg.qw + 2 * cfg.kvw
    assert qi_off % cfg.iw == 0 and qkw % cfg.kvw == 0
    cs, sn, csi, sni = tabs
    tab_spec = pl.BlockSpec((tm, LANES), lambda i: (i % nb, 0))
    return pl.pallas_call(
        functools.partial(_rope_kernel, n_q, n_q + n_kv, cfg.iw // LANES, cfg.idx_dim, cfg.head_dim ** -0.5 * LOG2E),
        grid=(m // tm,),
        in_specs=[
            pl.BlockSpec((tm, qkw), lambda i: (i, 0)),
            pl.BlockSpec((tm, cfg.kvw), lambda i: (i, qkw // cfg.kvw)),
            pl.BlockSpec((tm, cfg.iw), lambda i: (i, qi_off // cfg.iw)),
            pl.BlockSpec((tm, TAIL_W), lambda i: (i, 0)),
            tab_spec, tab_spec, tab_spec, tab_spec,
        ],
        out_specs=[
            pl.BlockSpec((tm, qkw), lambda i: (i, 0)),
            pl.BlockSpec((tm, cfg.iw), lambda i: (i, 0)),
            pl.BlockSpec((tm, 2 * LANES), lambda i: (i, 0)),
            pl.BlockSpec((None, cfg.kvw, tm), lambda i: (i // nb, 0, i % nb)),
            pl.BlockSpec((None, TAIL_W, tm), lambda i: (i // nb, 0, i % nb)),
        ],
        out_shape=[
            jax.ShapeDtypeStruct((m, qkw), BF16),
            jax.ShapeDtypeStruct((m, cfg.iw), BF16),
            jax.ShapeDtypeStruct((m, 2 * LANES), BF16),
            jax.ShapeDtypeStruct((bsz, cfg.kvw, n_rows), BF16),
            jax.ShapeDtypeStruct((bsz, TAIL_W, n_rows), F32),
        ],
        compiler_params=_params("parallel"),
        name="rope",
    )(main, main, main, tail, cs, sn, csi, sni)


def _rope_tables(cfg, n_rows):
    pos = (jnp.arange(n_rows) - cfg.lead).astype(F32)

    def tab(dim):
        half = dim // 2
        inv = cfg.rope_theta ** (-jnp.arange(half, dtype=F32) / half)
        ang = pos[:, None] * inv[None, :]
        c, s = jnp.cos(ang), jnp.sin(ang)
        reps = LANES // dim
        return jnp.tile(jnp.concatenate([c, c], -1), (1, reps)), jnp.tile(jnp.concatenate([-s, s], -1), (1, reps))

    cs, sn = tab(cfg.head_dim)
    csi, sni = tab(cfg.idx_dim)
    return cs, sn, csi, sni


NT = (((1,), (1,)), ((), ()))
NEG_INF_KEY = -2139095041
CNT_CHAINS = 4


def _dsa_kernel(cfg, topk, qi_ref, wt_ref, ki_ref, q_ref, k_ref, vt_ref, o_ref, key_ref, qs_ref, acc_ref, m_ref, l_ref,
                al_ref, s_ref, p_ref):
    ch = cfg.chunk
    c = pl.program_id(1) + 1
    n_kc = c + 1
    rep = cfg.attn_heads // cfg.kv_heads
    hd = cfg.head_dim
    row0 = c * ch
    int_min = jnp.int32(-2 ** 31)
    sub8 = ch // 8

    q_row = row0 + lax.broadcasted_iota(jnp.int32, (ch, ch), 1)
    k_sub = lax.broadcasted_iota(jnp.int32, (ch, ch), 0)

    def admissible(koff):
        k_row = koff + k_sub
        return (k_row >= cfg.lead) & (k_row <= q_row)

    w_all = wt_ref[...] * (cfg.idx_heads ** -0.5 * cfg.idx_dim ** -0.5)

    def score_chunk(kc, carry):
        koff = pl.multiple_of(kc * ch, ch)
        k_lo = ki_ref[pl.ds(koff, ch), 0:LANES]
        k_hi = ki_ref[pl.ds(koff, ch), LANES:2 * LANES]
        acc = jnp.zeros((ch, ch), F32)
        for p in range(cfg.idx_heads // 2):
            qp = qi_ref[:, p * LANES:(p + 1) * LANES]
            for half, kk in ((0, k_lo), (1, k_hi)):
                h = 2 * p + half
                lg = lax.dot_general(kk, qp, NT, preferred_element_type=F32)
                acc = acc + jnp.maximum(lg, 0.0) * w_all[h:h + 1, :]
        sc = jnp.where(admissible(koff), acc, -jnp.inf)
        bits = lax.bitcast_convert_type(sc, jnp.int32)
        key_ref[pl.ds(koff, ch), :] = bits ^ ((bits >> 31) & jnp.int32(0x7FFFFFFF))
        return carry

    lax.fori_loop(0, n_kc, score_chunk, 0)

    key_ref[pl.ds(pl.multiple_of(n_kc * ch, ch), ch), :] = jnp.full((ch, ch), NEG_INF_KEY, jnp.int32)

    def bit_step(it, tau_u):
        cand_u = tau_u | lax.shift_left(jnp.int32(1), jnp.int32(31) - it)
        cand = cand_u ^ int_min

        def cnt_chunk(kc2, cnt):
            koff = pl.multiple_of(kc2 * (2 * ch), 2 * ch)
            ind = jnp.where(key_ref[pl.ds(koff, 2 * ch), :] >= cand, 1.0, 0.0)
            return cnt + jnp.sum(ind.reshape(2 * sub8 // CNT_CHAINS, CNT_CHAINS, 8, ch), axis=0)

        cnt = lax.fori_loop(0, (n_kc + 1) // 2, cnt_chunk, jnp.zeros((CNT_CHAINS, 8, ch), F32))
        tot = jnp.sum(jnp.sum(cnt, axis=0), axis=0, keepdims=True)
        return jnp.where(tot >= float(topk), cand_u, tau_u)

    tau_u = lax.fori_loop(0, 32, bit_step, jnp.zeros((1, ch), jnp.int32))
    tau = tau_u ^ int_min

    def bias_chunk(kc, carry):
        koff = pl.multiple_of(kc * ch, ch)
        sel = (key_ref[pl.ds(koff, ch), :] >= tau) & admissible(koff)
        key_ref[pl.ds(koff, ch), :] = lax.bitcast_convert_type(jnp.where(sel, 0.0, NEG_BIG).astype(F32), jnp.int32)
        return carry

    lax.fori_loop(0, n_kc, bias_chunk, 0)

    for g in range(cfg.kv_heads):
        for r in range(rep):
            h = g * rep + r
            qs_ref[g, r * ch:(r + 1) * ch, :] = q_ref[:, h * hd:(h + 1) * hd]
    acc_ref[...] = jnp.zeros(acc_ref.shape, F32)
    m_ref[...] = jnp.full(m_ref.shape, NEG_BIG, F32)
    l_ref[...] = jnp.zeros(l_ref.shape, F32)

    def attn_chunk(kc, carry):
        koff = pl.multiple_of(kc * ch, ch)
        for g in range(cfg.kv_heads):
            kg = k_ref[pl.ds(koff, ch), g * hd:(g + 1) * hd]
            s_ref[...] = lax.dot_general(kg, qs_ref[g], NT, preferred_element_type=F32)
            for t in range(rep * ch // LANES):
                lanes = slice(t * LANES, (t + 1) * LANES)
                qlanes = slice((t * LANES) % ch, (t * LANES) % ch + LANES)
                s = s_ref[:, lanes] + lax.bitcast_convert_type(key_ref[pl.ds(koff, ch), qlanes], F32)
                m_old = m_ref[g, 0:1, lanes]
                m_new = jnp.maximum(m_old, jnp.max(s, axis=0, keepdims=True))
                alpha = jnp.exp2(m_old - m_new)
                p = jnp.exp2(s - m_new)
                l_ref[g, 0:1, lanes] = alpha * l_ref[g, 0:1, lanes] + jnp.sum(p, axis=0, keepdims=True)
                m_ref[g, 0:1, lanes] = m_new
                al_ref[0:1, lanes] = alpha
                p_ref[:, lanes] = p.astype(BF16)
            vtg = vt_ref[g * hd:(g + 1) * hd, pl.ds(koff, ch)]
            acc_ref[g] = al_ref[0:1, :] * acc_ref[g] + jnp.dot(vtg, p_ref[...], preferred_element_type=F32)
        return carry

    lax.fori_loop(0, n_kc, attn_chunk, 0)

    for g in range(cfg.kv_heads):
        og = acc_ref[g] / l_ref[g, 0:1, :]
        for r in range(rep):
            h = g * rep + r
            o_ref[:, h * hd:(h + 1) * hd] = og[:, r * ch:(r + 1) * ch].T.astype(o_ref.dtype)


def _dsa(cfg, topk, qk_r, qi_r, ki2, vt, tail_t, bsz, n_rows):
    ch = cfg.chunk
    nq = n_rows // ch - 1
    rep = cfg.attn_heads // cfg.kv_heads
    qk3 = qk_r.reshape(bsz, n_rows, -1)
    qi3 = qi_r.reshape(bsz, n_rows, -1)
    ki3 = ki2.reshape(bsz, n_rows, -1)
    assert cfg.qw % cfg.kvw == 0 and cfg.idx_dim % cfg.idx_heads == 0 and cfg.idx_heads % 8 == 0
    k_blk = cfg.qw // cfg.kvw
    return pl.pallas_call(
        functools.partial(_dsa_kernel, cfg, topk),
        grid=(bsz, nq),
        in_specs=[
            pl.BlockSpec((None, ch, cfg.iw), lambda b, i: (b, i + 1, 0)),
            pl.BlockSpec((None, cfg.idx_heads, ch), lambda b, i: (b, cfg.idx_dim // cfg.idx_heads, i + 1)),
            pl.BlockSpec((None, n_rows, 2 * LANES), lambda b, i: (b, 0, 0)),
            pl.BlockSpec((None, ch, cfg.qw), lambda b, i: (b, i + 1, 0)),
            pl.BlockSpec((None, n_rows, cfg.kvw), lambda b, i: (b, 0, k_blk)),
            pl.BlockSpec((None, cfg.kvw, n_rows), lambda b, i: (b, 0, 0)),
        ],
        out_specs=pl.BlockSpec((None, ch, cfg.qw), lambda b, i: (b, i, 0)),
        out_shape=jax.ShapeDtypeStruct((bsz, nq * ch, cfg.qw), BF16),
        scratch_shapes=[
            pltpu.VMEM((n_rows + ch, ch), jnp.int32),
            pltpu.VMEM((cfg.kv_heads, rep * ch, cfg.head_dim), BF16),
            pltpu.VMEM((cfg.kv_heads, cfg.head_dim, rep * ch), F32),
            pltpu.VMEM((cfg.kv_heads, 8, rep * ch), F32),
            pltpu.VMEM((cfg.kv_heads, 8, rep * ch), F32),
            pltpu.VMEM((8, rep * ch), F32),
            pltpu.VMEM((ch, rep * ch), F32),
            pltpu.VMEM((ch, rep * ch), BF16),
        ],
        compiler_params=_params("parallel", "arbitrary"),
        name="dsa",
    )(qi3, tail_t, ki3, qk3, qk3, vt)


def _split3(a):
    hi = a.astype(BF16)
    r1 = a - hi.astype(F32)
    mid = r1.astype(BF16)
    lo = (r1 - mid.astype(F32)).astype(BF16)
    return hi, mid, lo


def _dot3_rhs(ones_lhs, a):
    return sum(jnp.dot(ones_lhs, t, preferred_element_type=F32) for t in _split3(a))


def _dot3_lhs(a, ones_rhs):
    return sum(jnp.dot(t, ones_rhs, preferred_element_type=F32) for t in _split3(a))


def _silu(x):
    return x * jax.nn.sigmoid(x)


def _ssd_kernel(cfg, z_ref, x_ref, b_ref, c_ref, dtc_ref, dtr_ref, cwx_ref, cwb_ref, cwc_ref, cbx_ref, cbb_ref, cbc_ref,
                dbc_ref, dbr_ref, alc_ref, alr_ref, dsk_ref, nw_ref, y_ref, state_ref, carry_ref, ext_ref):
    ch, gw, ns, hdim, hpg = cfg.chunk, cfg.gw, cfg.d_state, cfg.ssm_head_dim, cfg.hpg
    cwid = cfg.conv_width
    c = pl.program_id(1)
    g = pl.program_id(2)
    ew = gw + 2 * ns

    @pl.when(c == 0)
    def _():
        state_ref[g] = jnp.zeros((ns, gw), F32)
        carry_ref[g] = jnp.zeros((8, ew), F32)

    ext_ref[0:8, :] = carry_ref[g]
    ext_ref[8:8 + ch, 0:gw] = x_ref[...].astype(F32)
    ext_ref[8:8 + ch, gw:gw + ns] = b_ref[...].astype(F32)
    ext_ref[8:8 + ch, gw + ns:ew] = c_ref[...].astype(F32)
    carry_ref[g] = ext_ref[ch:ch + 8, :]

    def conv(lo, hi, cw_ref, cb_ref):
        acc = jnp.zeros((ch, hi - lo), F32) + cb_ref[...]
        for w in range(cwid):
            acc = acc + ext_ref[8 - (cwid - 1) + w:8 - (cwid - 1) + w + ch, lo:hi] * cw_ref[w:w + 1, :]
        return _silu(acc)

    xc = conv(0, gw, cwx_ref, cbx_ref)
    bc = conv(gw, gw + ns, cwb_ref, cbb_ref)
    cc = conv(gw + ns, ew, cwc_ref, cbc_ref)

    rowi = lax.broadcasted_iota(jnp.int32, (ch, hpg), 0)
    coli = lax.broadcasted_iota(jnp.int32, (hpg, ch), 1)
    is_pad_c = (rowi < cfg.lead) & (c == 0)
    is_pad_r = (coli < cfg.lead) & (c == 0)
    dt_c = jnp.where(is_pad_c, 0.0, jax.nn.softplus(dtc_ref[...] + dbc_ref[...]))
    dt_r = jnp.where(is_pad_r, 0.0, jax.nn.softplus(dtr_ref[...] + dbr_ref[...]))
    a_c = dt_c * (-jnp.exp(alc_ref[...]))
    a_r = dt_r * (-jnp.exp(alr_ref[...]))
    li = lax.broadcasted_iota(jnp.int32, (ch, ch), 0)
    si = lax.broadcasted_iota(jnp.int32, (ch, ch), 1)
    causal = li >= si
    tri = jnp.where(causal, 1.0, 0.0).astype(BF16)
    tri_t = jnp.where(li <= si, 1.0, 0.0).astype(BF16)
    acs_c = _dot3_rhs(tri, a_c)
    acs_r = _dot3_lhs(a_r, tri_t)
    a_end = acs_r[:, ch - 1:ch]
    wdec_r = dt_r * jnp.exp(a_end - acs_r)

    cb = lax.dot_general(cc.astype(BF16), bc.astype(BF16), NT, preferred_element_type=F32)
    bt = bc.T
    xb = xc.astype(BF16)
    prev = state_ref[g]
    prev_b = prev.astype(BF16)
    lane_p = lax.broadcasted_iota(jnp.int32, (ch, 2 * hdim), 1)
    lane_s = lax.broadcasted_iota(jnp.int32, (ns, 2 * hdim), 1)

    y_pairs = []
    st_pairs = []
    for pr in range(hpg // 2):
        psl = slice(pr * 2 * hdim, (pr + 1) * 2 * hdim)
        x_pair = xb[:, psl]
        prev_pair = prev_b[:, psl]
        ys, sts = [], []
        for half in range(2):
            j = 2 * pr + half
            col = acs_c[:, j:j + 1]
            row = acs_r[j:j + 1, :]
            lmat = jnp.where(causal, jnp.exp(jnp.where(causal, col - row, 0.0)), 0.0)
            mmat = (cb * lmat * dt_r[j:j + 1, :]).astype(BF16)
            ce = (cc * jnp.exp(col)).astype(BF16)
            ys.append(jnp.dot(mmat, x_pair, preferred_element_type=F32)
                      + jnp.dot(ce, prev_pair, preferred_element_type=F32))
            btw = (bt * wdec_r[j:j + 1, :]).astype(BF16)
            sts.append(jnp.dot(btw, x_pair, preferred_element_type=F32))
        y_pairs.append(jnp.where(lane_p < hdim, ys[0], ys[1]))
        st_pairs.append(jnp.where(lane_s < hdim, sts[0], sts[1]))
    y = jnp.concatenate(y_pairs, axis=1)
    st = jnp.concatenate(st_pairs, axis=1)

    lane_g = lax.broadcasted_iota(jnp.int32, (1, gw), 1) // hdim
    cdec = jnp.zeros((1, gw), F32)
    for j in range(hpg):
        cdec = jnp.where(lane_g == j, jnp.exp(a_end[j:j + 1, :]), cdec)
    state_ref[g] = prev * cdec + st

    y = y + xc * dsk_ref[...]
    y = y * _silu(z_ref[...].astype(F32))
    ms = jnp.mean(y * y, axis=-1, keepdims=True)
    y_ref[...] = (y * lax.rsqrt(ms + cfg.eps) * nw_ref[...]).astype(y_ref.dtype)


def _ssd(cfg, main, dt_raw, tail_t, conv_w, conv_b, dt_bias, a_log, d_skip, norm_w, bsz, n_rows):
    ch, gw, ns, hpg, ng = cfg.chunk, cfg.gw, cfg.d_state, cfg.hpg, cfg.ssm_groups
    nc = n_rows // ch
    di = cfg.d_inner
    main3 = main.reshape(bsz, n_rows, -1)
    z_off = cfg.qw + 2 * cfg.kvw + cfg.iw
    x_off = z_off + di
    b_off = x_off + di
    c_off = b_off + cfg.bcw
    assert z_off % gw == 0 and b_off % ns == 0 and hpg % 2 == 0 and 2 * cfg.ssm_head_dim == LANES
    dtc = dt_raw.reshape(bsz, n_rows, ng, hpg).transpose(0, 2, 1, 3)
    cwx, cwb, cwc = conv_w[:, :di], conv_w[:, di:di + cfg.bcw], conv_w[:, di + cfg.bcw:]
    cb2 = conv_b.reshape(1, -1)
    cbx, cbb, cbc = cb2[:, :di], cb2[:, di:di + cfg.bcw], cb2[:, di + cfg.bcw:]
    ew = gw + 2 * ns
    cw = cfg.conv_width
    grp = lambda b, c, g: (b, c, g)
    return pl.pallas_call(
        functools.partial(_ssd_kernel, cfg),
        grid=(bsz, nc, ng),
        in_specs=[
            pl.BlockSpec((None, ch, gw), lambda b, c, g: (b, c, z_off // gw + g)),
            pl.BlockSpec((None, ch, gw), lambda b, c, g: (b, c, x_off // gw + g)),
            pl.BlockSpec((None, ch, ns), lambda b, c, g: (b, c, b_off // ns + g)),
            pl.BlockSpec((None, ch, ns), lambda b, c, g: (b, c, c_off // ns + g)),
            pl.BlockSpec((None, None, ch, hpg), lambda b, c, g: (b, g, c, 0)),
            pl.BlockSpec((None, hpg, ch), lambda b, c, g: (b, TAIL_DT // hpg + g, c)),
            pl.BlockSpec((cw, gw), lambda b, c, g: (0, g)),
            pl.BlockSpec((cw, ns), lambda b, c, g: (0, g)),
            pl.BlockSpec((cw, ns), lambda b, c, g: (0, g)),
            pl.BlockSpec((1, gw), lambda b, c, g: (0, g)),
            pl.BlockSpec((1, ns), lambda b, c, g: (0, g)),
            pl.BlockSpec((1, ns), lambda b, c, g: (0, g)),
            pl.BlockSpec((None, 1, hpg), lambda b, c, g: (g, 0, 0)),
            pl.BlockSpec((None, hpg, 1), lambda b, c, g: (g, 0, 0)),
            pl.BlockSpec((None, 1, hpg), lambda b, c, g: (g, 0, 0)),
            pl.BlockSpec((None, hpg, 1), lambda b, c, g: (g, 0, 0)),
            pl.BlockSpec((1, gw), lambda b, c, g: (0, g)),
            pl.BlockSpec((1, gw), lambda b, c, g: (0, g)),
        ],
        out_specs=pl.BlockSpec((None, ch, gw), grp),
        out_shape=jax.ShapeDtypeStruct((bsz, n_rows, di), BF16),
        scratch_shapes=[
            pltpu.VMEM((ng, ns, gw), F32),
            pltpu.VMEM((ng, 8, ew), F32),
            pltpu.VMEM((ch + 8, ew), F32),
        ],
        compiler_params=_params("parallel", "arbitrary", "arbitrary"),
        name="ssd",
    )(main3, main3, main3, main3, dtc, tail_t, cwx, cwb, cwc, cbx, cbb, cbc,
      dt_bias.reshape(ng, 1, hpg), dt_bias.reshape(ng, hpg, 1), a_log.reshape(ng, 1, hpg), a_log.reshape(ng, hpg, 1),
      jnp.repeat(d_skip, cfg.ssm_head_dim).reshape(1, di), norm_w.reshape(1, di))


def _merge_kernel(attn_ref, ssm_ref, ga_ref, gs_ref, wa_ref, ws_ref, o_ref):
    pa = jnp.dot(attn_ref[...], wa_ref[...], preferred_element_type=F32)
    ps = jnp.dot(ssm_ref[...], ws_ref[...], preferred_element_type=F32)
    ga = jax.nn.sigmoid(ga_ref[...].astype(F32))
    gs = jax.nn.sigmoid(gs_ref[...].astype(F32))
    o_ref[...] = (ga * pa + gs * ps).astype(o_ref.dtype)


def _merge(cfg, attn, ssm, main, w_a, w_s, bsz, n_rows):
    ch, d = cfg.chunk, cfg.d_model
    nq = n_rows // ch - 1
    tn = _pick(d, (1024, 512, 256))
    main3 = main.reshape(bsz, n_rows, -1)
    g_off = cfg.qw + 2 * cfg.kvw + cfg.iw + 2 * cfg.d_inner + 2 * cfg.bcw
    assert g_off % tn == 0
    return pl.pallas_call(
        _merge_kernel,
        grid=(d // tn, bsz, nq),
        in_specs=[
            pl.BlockSpec((None, ch, cfg.qw), lambda j, b, i: (b, i, 0)),
            pl.BlockSpec((None, ch, cfg.d_inner), lambda j, b, i: (b, i + 1, 0)),
            pl.BlockSpec((None, ch, tn), lambda j, b, i: (b, i + 1, g_off // tn + j)),
            pl.BlockSpec((None, ch, tn), lambda j, b, i: (b, i + 1, (g_off + d) // tn + j)),
            pl.BlockSpec((cfg.qw, tn), lambda j, b, i: (0, j)),
            pl.BlockSpec((cfg.d_inner, tn), lambda j, b, i: (0, j)),
        ],
        out_specs=pl.BlockSpec((None, ch, tn), lambda j, b, i: (b, i, j)),
        out_shape=jax.ShapeDtypeStruct((bsz, nq * ch, d), BF16),
        compiler_params=_params("parallel", "parallel", "parallel"),
        name="merge",
    )(attn, ssm, main3, main3, w_a, w_s)


def _outproj_kernel(eps, m_ref, h_ref, w_ref, nw_ref, h2_ref, hn_ref, hnt_ref):
    h2 = h_ref[...] + jnp.dot(m_ref[...], w_ref[...], preferred_element_type=F32)
    h2_ref[...] = h2
    ms = jnp.mean(h2 * h2, axis=-1, keepdims=True)
    hn = h2 * lax.rsqrt(ms + eps) * nw_ref[...]
    hn_ref[...] = hn.astype(hn_ref.dtype)
    hnt_ref[...] = hn.T.astype(hnt_ref.dtype)


def _outproj(cfg, merged, h3, w_out, norm_w, bsz, n_rows):
    ch, d = cfg.chunk, cfg.d_model
    nq = n_rows // ch - 1
    return pl.pallas_call(
        functools.partial(_outproj_kernel, cfg.eps),
        grid=(bsz, nq),
        in_specs=[
            pl.BlockSpec((None, ch, d), lambda b, i: (b, i, 0)),
            pl.BlockSpec((None, ch, d), lambda b, i: (b, i + 1, 0)),
            pl.BlockSpec((d, d), lambda b, i: (0, 0)),
            pl.BlockSpec((1, d), lambda b, i: (0, 0)),
        ],
        out_specs=[
            pl.BlockSpec((None, ch, d), lambda b, i: (b, i, 0)),
            pl.BlockSpec((None, ch, d), lambda b, i: (b, i, 0)),
            pl.BlockSpec((d, ch), lambda b, i: (0, b * nq + i)),
        ],
        out_shape=[
            jax.ShapeDtypeStruct((bsz, nq * ch, d), F32),
            jax.ShapeDtypeStruct((bsz, nq * ch, d), BF16),
            jax.ShapeDtypeStruct((d, bsz * nq * ch), BF16),
        ],
        compiler_params=_params("parallel", "parallel"),
        name="outproj",
    )(merged, h3, w_out, norm_w.reshape(1, d))


def _topk_desc(s, k):
    out = []
    for _ in range(k):
        m = jnp.max(s, axis=0, keepdims=True)
        out.append(m)
        s = jnp.where(s >= m, -jnp.inf, s)
    return out


def _route_kernel(cfg, x_ref, wq_ref, keys_ref, thr_ref, e0_ref, s1_ref, e1_ref):
    nk, k = cfg.n_keys, cfg.peer_topk
    half = cfg.peer_key_dim // 2
    q = jnp.dot(x_ref[...], wq_ref[...], preferred_element_type=F32)
    for h in range(cfg.peer_heads):
        st = []
        for cidx in range(2):
            off = (2 * h + cidx) * half
            qc = q[:, off:off + half].astype(BF16)
            st.append(lax.dot_general(keys_ref[cidx], qc, NT, preferred_element_type=F32))
        t0 = _topk_desc(st[0], k)
        t1 = _topk_desc(st[1], k)
        t1m = jnp.concatenate(t1, axis=0)
        cands = [a + t1m for a in t0]
        best = _topk_desc(jnp.concatenate(cands, axis=0), k)
        z = sum(jnp.exp(b - best[0]) for b in best)
        thr_i = jnp.full(st[0].shape, jnp.inf, F32)
        for r in range(k):
            lim = jnp.min(jnp.where(cands[r] >= best[k - 1], t1m, jnp.inf), axis=0, keepdims=True)
            thr_i = jnp.where(st[0] == t0[r], lim, thr_i)
        thr_ref[h] = thr_i
        e0_ref[h] = jnp.exp(st[0] - t0[0])
        s1_ref[h] = st[1]
        e1_ref[h] = jnp.exp(st[1] - t1[0]) / z


def _route(cfg, hn2, wq, keys):
    n, d = hn2.shape
    tm = 256
    nh, nk = cfg.peer_heads, cfg.n_keys
    qd = nh * cfg.peer_key_dim
    assert nk == LANES and cfg.peer_key_dim // 2 == LANES
    big = pl.BlockSpec((nh, nk, tm), lambda i: (0, 0, i))
    shp = jax.ShapeDtypeStruct((nh, nk, n), F32)
    return pl.pallas_call(
        functools.partial(_route_kernel, cfg),
        grid=(n // tm,),
        in_specs=[
            pl.BlockSpec((tm, d), lambda i: (i, 0)),
            pl.BlockSpec((d, qd), lambda i: (0, 0)),
            pl.BlockSpec((2, nk, cfg.peer_key_dim // 2), lambda i: (0, 0, 0)),
        ],
        out_specs=[big, big, big, big],
        out_shape=[shp, shp, shp, shp],
        compiler_params=_params("parallel"),
        name="peer_route",
    )(hn2, wq, keys)


def _gelu(x):
    return 0.5 * x * (1.0 + lax.erf(x * (2.0 ** -0.5)))


def _gate_tile(cfg, i, lanes, thr_ref, e0_ref, s1_ref, e1_ref):
    w = None
    for h in range(cfg.peer_heads):
        sel = s1_ref[h, :, lanes] >= thr_ref[h, i:i + 1, lanes]
        term = jnp.where(sel, e1_ref[h, :, lanes], 0.0) * e0_ref[h, i:i + 1, lanes]
        w = term if w is None else w + term
    return w


MXU_ROWS = 256


def _peer_kernel(cfg, ic, n_e, xt_ref, u_ref, vt_ref, thr_ref, e0_ref, s1_ref, e1_ref, o_ref, a_ref, hb_ref):
    nk = cfg.n_keys
    d, tm = o_ref.shape
    ec = ic * nk
    s = pl.program_id(0)
    e_down = jnp.maximum(s - 2, 0) % n_e
    slot = s % 2

    @pl.when(s == 0)
    def _():
        a_ref[1] = jnp.zeros(a_ref.shape[1:], F32)
        hb_ref[...] = jnp.zeros(hb_ref.shape, BF16)

    @pl.when(e_down == 0)
    def _():
        o_ref[...] = jnp.zeros(o_ref.shape, F32)

    def up(m0):
        rows = slice(m0, m0 + MXU_ROWS)
        a_ref[slot, rows, :] = jnp.dot(u_ref[rows, :], xt_ref[...], preferred_element_type=F32)

    def down(m0):
        rows = slice(m0, m0 + MXU_ROWS)
        o_ref[rows, :] += jnp.dot(vt_ref[rows, :], hb_ref[slot], preferred_element_type=F32)

    def gate(i, c):
        rows, lanes = slice(i * nk, (i + 1) * nk), slice(c * LANES, (c + 1) * LANES)
        w = _gate_tile(cfg, i, lanes, thr_ref, e0_ref, s1_ref, e1_ref)
        hb_ref[1 - slot, rows, lanes] = (w * _gelu(a_ref[1 - slot, rows, lanes])).astype(BF16)

    mxu = [functools.partial(up, m0) for m0 in range(0, ec, MXU_ROWS)]
    mxu += [functools.partial(down, m0) for m0 in range(0, d, MXU_ROWS)]
    vec = [functools.partial(gate, i, c) for i in range(ic) for c in range(tm // LANES)]
    done = 0
    for k, piece in enumerate(mxu):
        piece()
        upto = (k + 1) * len(vec) // len(mxu)
        for tile in vec[done:upto]:
            tile()
        done = upto


def _peer(cfg, hn2t, u_b, vt_b, routes):
    d, n = hn2t.shape
    nh, nk = cfg.peer_heads, cfg.n_keys
    tm = _pick(n, (512, 256))
    ic = 8
    ec = ic * nk
    n_e = nk // ic
    n_steps = (n // tm) * n_e
    assert ec % MXU_ROWS == 0 and d % MXU_ROWS == 0
    back = lambda s, k: jnp.clip(s - k, 0, n_steps - 1)
    chunked = pl.BlockSpec((nh, ic, tm), lambda s: (0, back(s, 1) % n_e, back(s, 1) // n_e))
    full = pl.BlockSpec((nh, nk, tm), lambda s: (0, 0, back(s, 1) // n_e))
    return pl.pallas_call(
        functools.partial(_peer_kernel, cfg, ic, n_e),
        grid=(n_steps + 2,),
        in_specs=[
            pl.BlockSpec((d, tm), lambda s: (0, back(s, 0) // n_e)),
            pl.BlockSpec((ec, d), lambda s: (back(s, 0) % n_e, 0)),
            pl.BlockSpec((d, ec), lambda s: (0, back(s, 2) % n_e)),
            chunked, chunked, full, full,
        ],
        out_specs=pl.BlockSpec((d, tm), lambda s: (0, back(s, 2) // n_e)),
        out_shape=jax.ShapeDtypeStruct((d, n), F32),
        scratch_shapes=[pltpu.VMEM((2, ec, tm), F32), pltpu.VMEM((2, ec, tm), BF16)],
        compiler_params=_params("arbitrary"),
        name="peer_experts",
    )(hn2t, u_b, vt_b, *routes)


def _final_kernel(eps, h2_ref, pt_ref, nw_ref, o_ref):
    h = h2_ref[...] + pt_ref[...].T
    ms = jnp.mean(h * h, axis=-1, keepdims=True)
    o_ref[...] = h * lax.rsqrt(ms + eps) * nw_ref[...]


def _final(cfg, h2, peer_t, norm_w):
    n, d = h2.shape
    tm = 256
    return pl.pallas_call(
        functools.partial(_final_kernel, cfg.eps),
        grid=(n // tm,),
        in_specs=[
            pl.BlockSpec((tm, d), lambda i: (i, 0)),
            pl.BlockSpec((d, tm), lambda i: (0, i)),
            pl.BlockSpec((1, d), lambda i: (0, 0)),
        ],
        out_specs=pl.BlockSpec((tm, d), lambda i: (i, 0)),
        out_shape=jax.ShapeDtypeStruct((n, d), F32),
        compiler_params=_params("parallel"),
        name="final_norm",
    )(h2, peer_t, norm_w.reshape(1, d))


def _layer(cfg, x, meta_tokens, norm_mix_w, w_in, conv_w, conv_b, dt_bias, a_log, d_skip, ssm_norm_w,
           w_branch_attn, w_branch_ssm, w_out, norm_ffn_w, peer_w_query, peer_sub_keys, peer_u, peer_v,
           norm_final_w):
    bsz, seq, d = x.shape
    ch = cfg.chunk
    assert seq % ch == 0 and norm_mix_w.shape[0] == 1
    n_rows = ch + seq
    topk = min(cfg.topk_max, seq // 4)
    h = jnp.concatenate([
        jnp.zeros((bsz, cfg.lead, d), x.dtype),
        jnp.broadcast_to(meta_tokens.astype(x.dtype)[None], (bsz, cfg.n_meta, d)),
        x], axis=1)

    w = w_in[0]
    o_iq = cfg.qw + 2 * cfg.kvw
    o_ik = o_iq + cfg.iw
    o_iw = o_ik + cfg.idx_dim
    o_z = o_iw + cfg.idx_heads
    o_dt = o_z + 2 * cfg.d_inner + 2 * cfg.bcw
    o_g = o_dt + cfg.ssm_heads
    w_main = jnp.concatenate([w[:, :o_ik], w[:, o_z:o_dt], w[:, o_g:]], axis=1).astype(BF16)
    zpad = lambda n: jnp.zeros((d, n), w.dtype)
    w_tail = jnp.concatenate([
        w[:, o_ik:o_z], zpad(TAIL_DT - cfg.idx_dim - cfg.idx_heads),
        w[:, o_dt:o_g], zpad(TAIL_W - TAIL_DT - cfg.ssm_heads)], axis=1).astype(BF16)

    main, tail = _inproj(cfg, h.reshape(bsz * n_rows, d), norm_mix_w[0], w_main, w_tail)
    qk_r, qi_r, ki2, vt, tail_t = _rope(cfg, main, tail, _rope_tables(cfg, n_rows), bsz, n_rows)
    attn = _dsa(cfg, topk, qk_r, qi_r, ki2, vt, tail_t, bsz, n_rows)
    dt_raw = tail.reshape(bsz, n_rows, TAIL_W)[:, :, TAIL_DT:TAIL_DT + cfg.ssm_heads]
    ssm = _ssd(cfg, main, dt_raw, tail_t, conv_w[0], conv_b[0], dt_bias[0], a_log[0], d_skip[0], ssm_norm_w[0], bsz, n_rows)
    merged = _merge(cfg, attn, ssm, main, w_branch_attn[0].astype(BF16), w_branch_ssm[0].astype(BF16), bsz, n_rows)
    h2, hn2, hn2t = _outproj(cfg, merged, h, w_out[0].astype(BF16), norm_ffn_w[0], bsz, n_rows)
    h2 = h2.reshape(bsz * seq, d)
    hn2 = hn2.reshape(bsz * seq, d)
    routes = _route(cfg, hn2, peer_w_query[0].astype(BF16), peer_sub_keys[0].astype(BF16))
    peer_t = _peer(cfg, hn2t, peer_u[0].astype(BF16), peer_v[0].T.astype(BF16), routes)
    out = _final(cfg, h2, peer_t, norm_final_w)
    return out.reshape(bsz, seq, d)


def kernel(x, meta_tokens, norm_mix_w, w_in, conv_w, conv_b, dt_bias, a_log, d_skip, ssm_norm_w, w_branch_attn,
           w_branch_ssm, w_out, norm_ffn_w, peer_w_query, peer_sub_keys, peer_u, peer_v, norm_final_w):
    return _layer(Cfg(), x, meta_tokens, norm_mix_w, w_in, conv_w, conv_b, dt_bias, a_log, d_skip, ssm_norm_w,
                  w_branch_attn, w_branch_ssm, w_out, norm_ffn_w, peer_w_query, peer_sub_keys, peer_u, peer_v,
                  norm_final_w)
```

```python
import functools
import math
from typing import NamedTuple

import jax
import jax.numpy as jnp
from jax import lax
from jax.experimental import pallas as pl
from jax.experimental.pallas import tpu as pltpu

F32 = jnp.float32
BF16 = jnp.bfloat16
LANES = 128
LOG2E = math.log2(math.e)
NEG_BIG = -1e30
VMEM_LIMIT = 56 * 1024 * 1024


class Cfg(NamedTuple):
    d_model: int = 2048
    n_meta: int = 16
    attn_heads: int = 16
    kv_heads: int = 4
    head_dim: int = 128
    idx_heads: int = 16
    idx_dim: int = 64
    topk_max: int = 256
    rope_theta: float = 10000.0
    d_inner: int = 4096
    ssm_head_dim: int = 64
    ssm_groups: int = 8
    d_state: int = 128
    conv_width: int = 4
    chunk: int = 256
    peer_heads: int = 8
    n_keys: int = 128
    peer_key_dim: int = 256
    peer_topk: int = 16
    eps: float = 1e-6

    @property
    def lead(self):
        return self.chunk - self.n_meta

    @property
    def qw(self):
        return self.attn_heads * self.head_dim

    @property
    def kvw(self):
        return self.kv_heads * self.head_dim

    @property
    def iw(self):
        return self.idx_heads * self.idx_dim

    @property
    def ssm_heads(self):
        return self.d_inner // self.ssm_head_dim

    @property
    def hpg(self):
        return self.ssm_heads // self.ssm_groups

    @property
    def gw(self):
        return self.d_inner // self.ssm_groups

    @property
    def bcw(self):
        return self.ssm_groups * self.d_state


TAIL_W = 256
TAIL_DT = 128


def _params(*sem):
    return pltpu.CompilerParams(dimension_semantics=sem, vmem_limit_bytes=VMEM_LIMIT)


def _rms(x, w, eps):
    ms = jnp.mean(x * x, axis=-1, keepdims=True)
    return x * lax.rsqrt(ms + eps) * w


def _prenorm_kernel(cfg, x_ref, meta_ref, nw_ref, o_ref):
    c = pl.program_id(1)

    @pl.when(c == 0)
    def _():
        o_ref[0:cfg.lead, :] = jnp.zeros((cfg.lead, o_ref.shape[1]), o_ref.dtype)
        o_ref[cfg.lead:cfg.chunk, :] = _rms(meta_ref[...], nw_ref[...], cfg.eps).astype(o_ref.dtype)

    @pl.when(c > 0)
    def _():
        o_ref[...] = _rms(x_ref[...], nw_ref[...], cfg.eps).astype(o_ref.dtype)


def _prenorm(cfg, x, meta_tokens, norm_w):
    bsz, seq, d = x.shape
    ch = cfg.chunk
    nc = seq // ch + 1
    return pl.pallas_call(
        functools.partial(_prenorm_kernel, cfg),
        grid=(bsz, nc),
        in_specs=[
            pl.BlockSpec((None, ch, d), lambda b, c: (b, jnp.maximum(c - 1, 0), 0)),
            pl.BlockSpec((cfg.n_meta, d), lambda b, c: (0, 0)),
            pl.BlockSpec((1, d), lambda b, c: (0, 0)),
        ],
        out_specs=pl.BlockSpec((None, ch, d), lambda b, c: (b, c, 0)),
        out_shape=jax.ShapeDtypeStruct((bsz, nc * ch, d), BF16),
        compiler_params=_params("parallel", "parallel"),
        name="prenorm",
    )(x, meta_tokens.astype(x.dtype), norm_w.reshape(1, d))


def _inproj_kernel(hn_ref, w_ref, wt_ref, main_ref, tail_ref):
    @pl.when(pl.program_id(1) == 0)
    def _():
        tail_ref[...] = jnp.dot(hn_ref[...], wt_ref[...], preferred_element_type=F32)

    main_ref[...] = jnp.dot(hn_ref[...], w_ref[...], preferred_element_type=F32).astype(main_ref.dtype)


def _pick(n, cands):
    for c in cands:
        if n % c == 0:
            return c
    raise ValueError(f"no tile for {n}")


def _inproj(cfg, hn2d, w_main, w_tail):
    m, d = hn2d.shape
    nm = w_main.shape[1]
    tm = _pick(m, (1088, 512, 256))
    tn = _pick(nm, (1024, 512, 256))
    return pl.pallas_call(
        _inproj_kernel,
        grid=(m // tm, nm // tn),
        in_specs=[
            pl.BlockSpec((tm, d), lambda i, j: (i, 0)),
            pl.BlockSpec((d, tn), lambda i, j: (0, j)),
            pl.BlockSpec((d, TAIL_W), lambda i, j: (0, 0)),
        ],
        out_specs=[
            pl.BlockSpec((tm, tn), lambda i, j: (i, j)),
            pl.BlockSpec((tm, TAIL_W), lambda i, j: (i, 0)),
        ],
        out_shape=[
            jax.ShapeDtypeStruct((m, nm), BF16),
            jax.ShapeDtypeStruct((m, TAIL_W), F32),
        ],
        compiler_params=_params("parallel", "arbitrary"),
        name="inproj",
    )(hn2d, w_main, w_tail)


def _rope_kernel(n_q, n_qk, n_ipair, idx_dim, q_scale, qk_ref, v_ref, qi_ref, tail_ref, cs_ref, sn_ref, csi_ref,
                 sni_ref, qk_out, qi_out, ki_out, vt_out, tt_out):
    cs = cs_ref[...]
    sn = sn_ref[...]
    for hd in range(n_qk):
        sl = slice(hd * LANES, (hd + 1) * LANES)
        x = qk_ref[:, sl].astype(F32)
        y = x * cs + pltpu.roll(x, LANES // 2, 1) * sn
        if hd < n_q:
            y = y * q_scale
        qk_out[:, sl] = y.astype(BF16)
    csi = csi_ref[...]
    sni = sni_ref[...]
    half = idx_dim // 2
    lane = lax.broadcasted_iota(jnp.int32, csi.shape, 1)
    first = (lane % idx_dim) < half

    def rope_idx(x):
        rot = jnp.where(first, pltpu.roll(x, LANES - half, 1), pltpu.roll(x, half, 1))
        return x * csi + rot * sni

    for p in range(n_ipair):
        sl = slice(p * LANES, (p + 1) * LANES)
        qi_out[:, sl] = rope_idx(qi_ref[:, sl].astype(F32)).astype(BF16)
    kt = rope_idx(tail_ref[:, 0:LANES])
    klo = jnp.where(lane < idx_dim, kt, 0.0)
    ki_out[:, 0:LANES] = klo.astype(BF16)
    ki_out[:, LANES:2 * LANES] = pltpu.roll(klo, idx_dim, 1).astype(BF16)
    vt_out[...] = v_ref[...].astype(F32).T.astype(BF16)
    tt_out[...] = tail_ref[...].T


def _rope(cfg, main, tail, tabs, bsz, n_rows):
    m = main.shape[0]
    tm = 256
    nb = n_rows // tm
    n_q, n_kv = cfg.attn_heads, cfg.kv_heads
    qkw = cfg.qw + cfg.kvw
    assert cfg.head_dim == LANES and 2 * cfg.idx_dim == LANES
    qi_off = cfg.qw + 2 * cfg.kvw
    assert qi_off % cfg.iw == 0 and qkw % cfg.kvw == 0
    cs, sn, csi, sni = tabs
    tab_spec = pl.BlockSpec((tm, LANES), lambda i: (i % nb, 0))
    return pl.pallas_call(
        functools.partial(_rope_kernel, n_q, n_q + n_kv, cfg.iw // LANES, cfg.idx_dim, cfg.head_dim ** -0.5 * LOG2E),
        grid=(m // tm,),
        in_specs=[
            pl.BlockSpec((tm, qkw), lambda i: (i, 0)),
            pl.BlockSpec((tm, cfg.kvw), lambda i: (i, qkw // cfg.kvw)),
            pl.BlockSpec((tm, cfg.iw), lambda i: (i, qi_off // cfg.iw)),
            pl.BlockSpec((tm, TAIL_W), lambda i: (i, 0)),
            tab_spec, tab_spec, tab_spec, tab_spec,
        ],
        out_specs=[
            pl.BlockSpec((tm, qkw), lambda i: (i, 0)),
            pl.BlockSpec((tm, cfg.iw), lambda i: (i, 0)),
            pl.BlockSpec((tm, 2 * LANES), lambda i: (i, 0)),
            pl.BlockSpec((None, cfg.kvw, tm), lambda i: (i // nb, 0, i % nb)),
            pl.BlockSpec((None, TAIL_W, tm), lambda i: (i // nb, 0, i % nb)),
        ],
        out_shape=[
            jax.ShapeDtypeStruct((m, qkw), BF16),
            jax.ShapeDtypeStruct((m, cfg.iw), BF16),
            jax.ShapeDtypeStruct((m, 2 * LANES), BF16),
            jax.ShapeDtypeStruct((bsz, cfg.kvw, n_rows), BF16),
            jax.ShapeDtypeStruct((bsz, TAIL_W, n_rows), F32),
        ],
        compiler_params=_params("parallel"),
        name="rope",
    )(main, main, main, tail, cs, sn, csi, sni)


def _rope_tables(cfg, n_rows):
    pos = (jnp.arange(n_rows) - cfg.lead).astype(F32)

    def tab(dim):
        half = dim // 2
        inv = cfg.rope_theta ** (-jnp.arange(half, dtype=F32) / half)
        ang = pos[:, None] * inv[None, :]
        c, s = jnp.cos(ang), jnp.sin(ang)
        reps = LANES // dim
        return jnp.tile(jnp.concatenate([c, c], -1), (1, reps)), jnp.tile(jnp.concatenate([-s, s], -1), (1, reps))

    cs, sn = tab(cfg.head_dim)
    csi, sni = tab(cfg.idx_dim)
    return cs, sn, csi, sni


NT = (((1,), (1,)), ((), ()))
NEG_INF_KEY = -2139095041
CNT_CHAINS = 4


def _dsa_kernel(cfg, topk, qi_ref, wt_ref, ki_ref, q_ref, k_ref, vt_ref, o_ref, key_ref, qs_ref, acc_ref, m_ref, l_ref,
                al_ref, s_ref, p_ref):
    ch = cfg.chunk
    c = pl.program_id(1) + 1
    n_kc = c + 1
    rep = cfg.attn_heads // cfg.kv_heads
    hd = cfg.head_dim
    row0 = c * ch
    int_min = jnp.int32(-2 ** 31)
    sub8 = ch // 8

    q_row = row0 + lax.broadcasted_iota(jnp.int32, (ch, ch), 1)
    k_sub = lax.broadcasted_iota(jnp.int32, (ch, ch), 0)

    def admissible(koff):
        k_row = koff + k_sub
        return (k_row >= cfg.lead) & (k_row <= q_row)

    w_all = wt_ref[...] * (cfg.idx_heads ** -0.5 * cfg.idx_dim ** -0.5)

    def score_chunk(kc, carry):
        koff = pl.multiple_of(kc * ch, ch)
        k_lo = ki_ref[pl.ds(koff, ch), 0:LANES]
        k_hi = ki_ref[pl.ds(koff, ch), LANES:2 * LANES]
        acc = jnp.zeros((ch, ch), F32)
        for p in range(cfg.idx_heads // 2):
            qp = qi_ref[:, p * LANES:(p + 1) * LANES]
            for half, kk in ((0, k_lo), (1, k_hi)):
                h = 2 * p + half
                lg = lax.dot_general(kk, qp, NT, preferred_element_type=F32)
                acc = acc + jnp.maximum(lg, 0.0) * w_all[h:h + 1, :]
        sc = jnp.where(admissible(koff), acc, -jnp.inf)
        bits = lax.bitcast_convert_type(sc, jnp.int32)
        key_ref[pl.ds(koff, ch), :] = bits ^ ((bits >> 31) & jnp.int32(0x7FFFFFFF))
        return carry

    lax.fori_loop(0, n_kc, score_chunk, 0)

    key_ref[pl.ds(pl.multiple_of(n_kc * ch, ch), ch), :] = jnp.full((ch, ch), NEG_INF_KEY, jnp.int32)

    def count(pred_of_keys):
        def cnt_chunk(kc2, cnt):
            koff = pl.multiple_of(kc2 * (2 * ch), 2 * ch)
            ind = jnp.where(pred_of_keys(key_ref[pl.ds(koff, 2 * ch), :]), 1.0, 0.0)
            return cnt + jnp.sum(ind.reshape(2 * sub8 // CNT_CHAINS, CNT_CHAINS, 8, ch), axis=0)

        cnt = lax.fori_loop(0, (n_kc + 1) // 2, cnt_chunk, jnp.zeros((CNT_CHAINS, 8, ch), F32))
        return jnp.sum(jnp.sum(cnt, axis=0), axis=0, keepdims=True)

    kf = float(topk)

    def bit_step(it, state):
        tau_u, n_ge = state
        cand_u = tau_u | lax.shift_left(jnp.int32(1), jnp.int32(31) - it)
        cand = cand_u ^ int_min
        tot = count(lambda kv: kv >= cand)
        take = tot >= kf
        return jnp.where(take, cand_u, tau_u), jnp.where(take, tot, n_ge)

    tau_u, n_ge = lax.fori_loop(0, 32, bit_step, (jnp.zeros((1, ch), jnp.int32), jnp.full((1, ch), 2.0 * kf, F32)))
    tau = tau_u ^ int_min

    tied = jnp.max(jnp.where((n_ge > kf) & (tau != NEG_INF_KEY), 1.0, 0.0)) > 0.0

    @pl.when(jnp.logical_not(tied))
    def _():
        def bias_chunk(kc, carry):
            koff = pl.multiple_of(kc * ch, ch)
            sel = (key_ref[pl.ds(koff, ch), :] >= tau) & admissible(koff)
            key_ref[pl.ds(koff, ch), :] = lax.bitcast_convert_type(jnp.where(sel, 0.0, NEG_BIG).astype(F32), jnp.int32)
            return carry

        lax.fori_loop(0, n_kc, bias_chunk, 0)

    @pl.when(tied)
    def _():
        need = kf - count(lambda kv: kv > tau)
        below = (k_sub >= q_row - row0).astype(BF16)

        def bias_chunk(kc, seen):
            koff = pl.multiple_of(kc * ch, ch)
            kv = key_ref[pl.ds(koff, ch), :]
            eq = kv == tau
            rank = seen + jnp.dot(below, jnp.where(eq, 1.0, 0.0).astype(BF16), preferred_element_type=F32)
            sel = ((kv > tau) | (eq & (rank <= need))) & admissible(koff)
            key_ref[pl.ds(koff, ch), :] = lax.bitcast_convert_type(jnp.where(sel, 0.0, NEG_BIG).astype(F32), jnp.int32)
            return rank[ch - 1:ch, :]

        lax.fori_loop(0, n_kc, bias_chunk, jnp.zeros((1, ch), F32))

    for g in range(cfg.kv_heads):
        for r in range(rep):
            h = g * rep + r
            qs_ref[g, r * ch:(r + 1) * ch, :] = q_ref[:, h * hd:(h + 1) * hd]
    acc_ref[...] = jnp.zeros(acc_ref.shape, F32)
    m_ref[...] = jnp.full(m_ref.shape, NEG_BIG, F32)
    l_ref[...] = jnp.zeros(l_ref.shape, F32)

    def attn_chunk(kc, carry):
        koff = pl.multiple_of(kc * ch, ch)

        def scores(g):
            kg = k_ref[pl.ds(koff, ch), g * hd:(g + 1) * hd]
            s_ref[g % 2] = lax.dot_general(kg, qs_ref[g], NT, preferred_element_type=F32)

        def softmax_pv(g):
            for t in range(rep * ch // LANES):
                lanes = slice(t * LANES, (t + 1) * LANES)
                qlanes = slice((t * LANES) % ch, (t * LANES) % ch + LANES)
                s = s_ref[g % 2, :, lanes] + lax.bitcast_convert_type(key_ref[pl.ds(koff, ch), qlanes], F32)
                m_old = m_ref[g, 0:1, lanes]
                m_new = jnp.maximum(m_old, jnp.max(s, axis=0, keepdims=True))
                alpha = jnp.exp2(m_old - m_new)
                p = jnp.exp2(s - m_new)
                l_ref[g, 0:1, lanes] = alpha * l_ref[g, 0:1, lanes] + jnp.sum(p, axis=0, keepdims=True)
                m_ref[g, 0:1, lanes] = m_new
                al_ref[g % 2, 0:1, lanes] = alpha
                p_ref[g % 2, :, lanes] = p.astype(BF16)
            vtg = vt_ref[g * hd:(g + 1) * hd, pl.ds(koff, ch)]
            acc_ref[g] = al_ref[g % 2, 0:1, :] * acc_ref[g] + jnp.dot(vtg, p_ref[g % 2], preferred_element_type=F32)

        scores(0)
        for g in range(cfg.kv_heads):
            if g + 1 < cfg.kv_heads:
                scores(g + 1)
            softmax_pv(g)
        return carry

    lax.fori_loop(0, n_kc, attn_chunk, 0)

    for g in range(cfg.kv_heads):
        og = acc_ref[g] / l_ref[g, 0:1, :]
        for r in range(rep):
            h = g * rep + r
            o_ref[:, h * hd:(h + 1) * hd] = og[:, r * ch:(r + 1) * ch].T.astype(o_ref.dtype)


def _dsa(cfg, topk, qk_r, qi_r, ki2, vt, tail_t, bsz, n_rows):
    ch = cfg.chunk
    nq = n_rows // ch - 1
    rep = cfg.attn_heads // cfg.kv_heads
    qk3 = qk_r.reshape(bsz, n_rows, -1)
    qi3 = qi_r.reshape(bsz, n_rows, -1)
    ki3 = ki2.reshape(bsz, n_rows, -1)
    assert cfg.qw % cfg.kvw == 0 and cfg.idx_dim % cfg.idx_heads == 0 and cfg.idx_heads % 8 == 0
    k_blk = cfg.qw // cfg.kvw
    return pl.pallas_call(
        functools.partial(_dsa_kernel, cfg, topk),
        grid=(bsz, nq),
        in_specs=[
            pl.BlockSpec((None, ch, cfg.iw), lambda b, i: (b, i + 1, 0)),
            pl.BlockSpec((None, cfg.idx_heads, ch), lambda b, i: (b, cfg.idx_dim // cfg.idx_heads, i + 1)),
            pl.BlockSpec((None, n_rows, 2 * LANES), lambda b, i: (b, 0, 0)),
            pl.BlockSpec((None, ch, cfg.qw), lambda b, i: (b, i + 1, 0)),
            pl.BlockSpec((None, n_rows, cfg.kvw), lambda b, i: (b, 0, k_blk)),
            pl.BlockSpec((None, cfg.kvw, n_rows), lambda b, i: (b, 0, 0)),
        ],
        out_specs=pl.BlockSpec((None, ch, cfg.qw), lambda b, i: (b, i, 0)),
        out_shape=jax.ShapeDtypeStruct((bsz, nq * ch, cfg.qw), BF16),
        scratch_shapes=[
            pltpu.VMEM((n_rows + ch, ch), jnp.int32),
            pltpu.VMEM((cfg.kv_heads, rep * ch, cfg.head_dim), BF16),
            pltpu.VMEM((cfg.kv_heads, cfg.head_dim, rep * ch), F32),
            pltpu.VMEM((cfg.kv_heads, 8, rep * ch), F32),
            pltpu.VMEM((cfg.kv_heads, 8, rep * ch), F32),
            pltpu.VMEM((2, 8, rep * ch), F32),
            pltpu.VMEM((2, ch, rep * ch), F32),
            pltpu.VMEM((2, ch, rep * ch), BF16),
        ],
        compiler_params=_params("parallel", "arbitrary"),
        name="dsa",
    )(qi3, tail_t, ki3, qk3, qk3, vt)


def _split3(a):
    hi = a.astype(BF16)
    r1 = a - hi.astype(F32)
    mid = r1.astype(BF16)
    lo = (r1 - mid.astype(F32)).astype(BF16)
    return hi, mid, lo


def _dot3_rhs(ones_lhs, a):
    return sum(jnp.dot(ones_lhs, t, preferred_element_type=F32) for t in _split3(a))


def _dot3_lhs(a, ones_rhs):
    return sum(jnp.dot(t, ones_rhs, preferred_element_type=F32) for t in _split3(a))


def _silu(x):
    return x * jax.nn.sigmoid(x)


def _ssd_kernel(cfg, z_ref, x_ref, b_ref, c_ref, dtc_ref, dtr_ref, cwx_ref, cwb_ref, cwc_ref, cbx_ref, cbb_ref, cbc_ref,
                dbc_ref, dbr_ref, alc_ref, alr_ref, dsk_ref, nw_ref, y_ref, state_ref, carry_ref, ext_ref):
    ch, gw, ns, hdim, hpg = cfg.chunk, cfg.gw, cfg.d_state, cfg.ssm_head_dim, cfg.hpg
    cwid = cfg.conv_width
    c = pl.program_id(1)
    g = pl.program_id(2)
    ew = gw + 2 * ns

    @pl.when(c == 0)
    def _():
        state_ref[g] = jnp.zeros((ns, gw), F32)
        carry_ref[g] = jnp.zeros((8, ew), F32)

    ext_ref[0:8, :] = carry_ref[g]
    ext_ref[8:8 + ch, 0:gw] = x_ref[...].astype(F32)
    ext_ref[8:8 + ch, gw:gw + ns] = b_ref[...].astype(F32)
    ext_ref[8:8 + ch, gw + ns:ew] = c_ref[...].astype(F32)
    carry_ref[g] = ext_ref[ch:ch + 8, :]

    def conv(lo, hi, cw_ref, cb_ref):
        acc = jnp.zeros((ch, hi - lo), F32) + cb_ref[...]
        for w in range(cwid):
            acc = acc + ext_ref[8 - (cwid - 1) + w:8 - (cwid - 1) + w + ch, lo:hi] * cw_ref[w:w + 1, :]
        return _silu(acc)

    xc = conv(0, gw, cwx_ref, cbx_ref)
    bc = conv(gw, gw + ns, cwb_ref, cbb_ref)
    cc = conv(gw + ns, ew, cwc_ref, cbc_ref)

    rowi = lax.broadcasted_iota(jnp.int32, (ch, hpg), 0)
    coli = lax.broadcasted_iota(jnp.int32, (hpg, ch), 1)
    is_pad_c = (rowi < cfg.lead) & (c == 0)
    is_pad_r = (coli < cfg.lead) & (c == 0)
    dt_c = jnp.where(is_pad_c, 0.0, jax.nn.softplus(dtc_ref[...] + dbc_ref[...]))
    dt_r = jnp.where(is_pad_r, 0.0, jax.nn.softplus(dtr_ref[...] + dbr_ref[...]))
    a_c = dt_c * (-jnp.exp(alc_ref[...]))
    a_r = dt_r * (-jnp.exp(alr_ref[...]))
    li = lax.broadcasted_iota(jnp.int32, (ch, ch), 0)
    si = lax.broadcasted_iota(jnp.int32, (ch, ch), 1)
    causal = li >= si
    tri = jnp.where(causal, 1.0, 0.0).astype(BF16)
    tri_t = jnp.where(li <= si, 1.0, 0.0).astype(BF16)
    acs_c = _dot3_rhs(tri, a_c)
    acs_r = _dot3_lhs(a_r, tri_t)
    a_end = acs_r[:, ch - 1:ch]
    wdec_r = dt_r * jnp.exp(a_end - acs_r)

    cb = lax.dot_general(cc.astype(BF16), bc.astype(BF16), NT, preferred_element_type=F32)
    cb = jnp.where(causal, cb, 0.0)
    bt = bc.T
    xb = xc.astype(BF16)
    prev = state_ref[g]
    prev_b = prev.astype(BF16)
    lane_p = lax.broadcasted_iota(jnp.int32, (ch, 2 * hdim), 1)
    lane_s = lax.broadcasted_iota(jnp.int32, (ns, 2 * hdim), 1)

    y_pairs = []
    st_pairs = []
    for pr in range(hpg // 2):
        psl = slice(pr * 2 * hdim, (pr + 1) * 2 * hdim)
        x_pair = xb[:, psl]
        prev_pair = prev_b[:, psl]
        ys, sts = [], []
        for half in range(2):
            j = 2 * pr + half
            col = acs_c[:, j:j + 1]
            row = acs_r[j:j + 1, :]
            lmat = jnp.exp(jnp.minimum(col - row, 0.0))
            mmat = (cb * lmat * dt_r[j:j + 1, :]).astype(BF16)
            ce = (cc * jnp.exp(col)).astype(BF16)
            ys.append(jnp.dot(mmat, x_pair, preferred_element_type=F32)
                      + jnp.dot(ce, prev_pair, preferred_element_type=F32))
            btw = (bt * wdec_r[j:j + 1, :]).astype(BF16)
            sts.append(jnp.dot(btw, x_pair, preferred_element_type=F32))
        y_pairs.append(jnp.where(lane_p < hdim, ys[0], ys[1]))
        st_pairs.append(jnp.where(lane_s < hdim, sts[0], sts[1]))
    y = jnp.concatenate(y_pairs, axis=1)
    st = jnp.concatenate(st_pairs, axis=1)

    lane_g = lax.broadcasted_iota(jnp.int32, (1, gw), 1) // hdim
    cdec = jnp.zeros((1, gw), F32)
    for j in range(hpg):
        cdec = jnp.where(lane_g == j, jnp.exp(a_end[j:j + 1, :]), cdec)
    state_ref[g] = prev * cdec + st

    y = y + xc * dsk_ref[...]
    y = y * _silu(z_ref[...].astype(F32))
    ms = jnp.mean(y * y, axis=-1, keepdims=True)
    y_ref[...] = (y * lax.rsqrt(ms + cfg.eps) * nw_ref[...]).astype(y_ref.dtype)


def _ssd(cfg, main, dt_raw, tail_t, conv_w, conv_b, dt_bias, a_log, d_skip, norm_w, bsz, n_rows):
    ch, gw, ns, hpg, ng = cfg.chunk, cfg.gw, cfg.d_state, cfg.hpg, cfg.ssm_groups
    nc = n_rows // ch
    di = cfg.d_inner
    main3 = main.reshape(bsz, n_rows, -1)
    z_off = cfg.qw + 2 * cfg.kvw + cfg.iw
    x_off = z_off + di
    b_off = x_off + di
    c_off = b_off + cfg.bcw
    assert z_off % gw == 0 and b_off % ns == 0 and hpg % 2 == 0 and 2 * cfg.ssm_head_dim == LANES
    dtc = dt_raw.reshape(bsz, n_rows, ng, hpg).transpose(0, 2, 1, 3)
    cwx, cwb, cwc = conv_w[:, :di], conv_w[:, di:di + cfg.bcw], conv_w[:, di + cfg.bcw:]
    cb2 = conv_b.reshape(1, -1)
    cbx, cbb, cbc = cb2[:, :di], cb2[:, di:di + cfg.bcw], cb2[:, di + cfg.bcw:]
    ew = gw + 2 * ns
    cw = cfg.conv_width
    grp = lambda b, c, g: (b, c, g)
    return pl.pallas_call(
        functools.partial(_ssd_kernel, cfg),
        grid=(bsz, nc, ng),
        in_specs=[
            pl.BlockSpec((None, ch, gw), lambda b, c, g: (b, c, z_off // gw + g)),
            pl.BlockSpec((None, ch, gw), lambda b, c, g: (b, c, x_off // gw + g)),
            pl.BlockSpec((None, ch, ns), lambda b, c, g: (b, c, b_off // ns + g)),
            pl.BlockSpec((None, ch, ns), lambda b, c, g: (b, c, c_off // ns + g)),
            pl.BlockSpec((None, None, ch, hpg), lambda b, c, g: (b, g, c, 0)),
            pl.BlockSpec((None, hpg, ch), lambda b, c, g: (b, TAIL_DT // hpg + g, c)),
            pl.BlockSpec((cw, gw), lambda b, c, g: (0, g)),
            pl.BlockSpec((cw, ns), lambda b, c, g: (0, g)),
            pl.BlockSpec((cw, ns), lambda b, c, g: (0, g)),
            pl.BlockSpec((1, gw), lambda b, c, g: (0, g)),
            pl.BlockSpec((1, ns), lambda b, c, g: (0, g)),
            pl.BlockSpec((1, ns), lambda b, c, g: (0, g)),
            pl.BlockSpec((None, 1, hpg), lambda b, c, g: (g, 0, 0)),
            pl.BlockSpec((None, hpg, 1), lambda b, c, g: (g, 0, 0)),
            pl.BlockSpec((None, 1, hpg), lambda b, c, g: (g, 0, 0)),
            pl.BlockSpec((None, hpg, 1), lambda b, c, g: (g, 0, 0)),
            pl.BlockSpec((1, gw), lambda b, c, g: (0, g)),
            pl.BlockSpec((1, gw), lambda b, c, g: (0, g)),
        ],
        out_specs=pl.BlockSpec((None, ch, gw), grp),
        out_shape=jax.ShapeDtypeStruct((bsz, n_rows, di), BF16),
        scratch_shapes=[
            pltpu.VMEM((ng, ns, gw), F32),
            pltpu.VMEM((ng, 8, ew), F32),
            pltpu.VMEM((ch + 8, ew), F32),
        ],
        compiler_params=_params("parallel", "arbitrary", "arbitrary"),
        name="ssd",
    )(main3, main3, main3, main3, dtc, tail_t, cwx, cwb, cwc, cbx, cbb, cbc,
      dt_bias.reshape(ng, 1, hpg), dt_bias.reshape(ng, hpg, 1), a_log.reshape(ng, 1, hpg), a_log.reshape(ng, hpg, 1),
      jnp.repeat(d_skip, cfg.ssm_head_dim).reshape(1, di), norm_w.reshape(1, di))


def _merge_kernel(attn_ref, ssm_ref, ga_ref, gs_ref, wa_ref, ws_ref, o_ref):
    pa = jnp.dot(attn_ref[...], wa_ref[...], preferred_element_type=F32)
    ps = jnp.dot(ssm_ref[...], ws_ref[...], preferred_element_type=F32)
    ga = jax.nn.sigmoid(ga_ref[...].astype(F32))
    gs = jax.nn.sigmoid(gs_ref[...].astype(F32))
    o_ref[...] = (ga * pa + gs * ps).astype(o_ref.dtype)


def _merge(cfg, attn, ssm, main, w_a, w_s, bsz, n_rows):
    ch, d = cfg.chunk, cfg.d_model
    nq = n_rows // ch - 1
    tn = _pick(d, (1024, 512, 256))
    main3 = main.reshape(bsz, n_rows, -1)
    g_off = cfg.qw + 2 * cfg.kvw + cfg.iw + 2 * cfg.d_inner + 2 * cfg.bcw
    assert g_off % tn == 0
    return pl.pallas_call(
        _merge_kernel,
        grid=(d // tn, bsz, nq),
        in_specs=[
            pl.BlockSpec((None, ch, cfg.qw), lambda j, b, i: (b, i, 0)),
            pl.BlockSpec((None, ch, cfg.d_inner), lambda j, b, i: (b, i + 1, 0)),
            pl.BlockSpec((None, ch, tn), lambda j, b, i: (b, i + 1, g_off // tn + j)),
            pl.BlockSpec((None, ch, tn), lambda j, b, i: (b, i + 1, (g_off + d) // tn + j)),
            pl.BlockSpec((cfg.qw, tn), lambda j, b, i: (0, j)),
            pl.BlockSpec((cfg.d_inner, tn), lambda j, b, i: (0, j)),
        ],
        out_specs=pl.BlockSpec((None, ch, tn), lambda j, b, i: (b, i, j)),
        out_shape=jax.ShapeDtypeStruct((bsz, nq * ch, d), BF16),
        compiler_params=_params("parallel", "parallel", "parallel"),
        name="merge",
    )(attn, ssm, main3, main3, w_a, w_s)


def _outproj_kernel(eps, m_ref, h_ref, w_ref, nw_ref, h2_ref, hn_ref, hnt_ref):
    h2 = h_ref[...] + jnp.dot(m_ref[...], w_ref[...], preferred_element_type=F32)
    h2_ref[...] = h2
    ms = jnp.mean(h2 * h2, axis=-1, keepdims=True)
    hn = h2 * lax.rsqrt(ms + eps) * nw_ref[...]
    hn_ref[...] = hn.astype(hn_ref.dtype)
    hnt_ref[...] = hn.T.astype(hnt_ref.dtype)


def _outproj(cfg, merged, x, w_out, norm_w, bsz, n_rows):
    ch, d = cfg.chunk, cfg.d_model
    nq = n_rows // ch - 1
    return pl.pallas_call(
        functools.partial(_outproj_kernel, cfg.eps),
        grid=(bsz, nq),
        in_specs=[
            pl.BlockSpec((None, ch, d), lambda b, i: (b, i, 0)),
            pl.BlockSpec((None, ch, d), lambda b, i: (b, i, 0)),
            pl.BlockSpec((d, d), lambda b, i: (0, 0)),
            pl.BlockSpec((1, d), lambda b, i: (0, 0)),
        ],
        out_specs=[
            pl.BlockSpec((None, ch, d), lambda b, i: (b, i, 0)),
            pl.BlockSpec((None, ch, d), lambda b, i: (b, i, 0)),
            pl.BlockSpec((d, ch), lambda b, i: (0, b * nq + i)),
        ],
        out_shape=[
            jax.ShapeDtypeStruct((bsz, nq * ch, d), F32),
            jax.ShapeDtypeStruct((bsz, nq * ch, d), BF16),
            jax.ShapeDtypeStruct((d, bsz * nq * ch), BF16),
        ],
        compiler_params=_params("parallel", "parallel"),
        name="outproj",
    )(merged, x, w_out, norm_w.reshape(1, d))


def _topk_desc(s, k):
    out = []
    for _ in range(k):
        m = jnp.max(s, axis=0, keepdims=True)
        out.append(m)
        s = jnp.where(s >= m, -jnp.inf, s)
    return out


def _route_kernel(cfg, x_ref, wq_ref, keys_ref, thr_ref, e0_ref, s1_ref, e1_ref):
    nk, k = cfg.n_keys, cfg.peer_topk
    half = cfg.peer_key_dim // 2
    q = jnp.dot(x_ref[...], wq_ref[...], preferred_element_type=F32)
    for h in range(cfg.peer_heads):
        st = []
        for cidx in range(2):
            off = (2 * h + cidx) * half
            qc = q[:, off:off + half].astype(BF16)
            st.append(lax.dot_general(keys_ref[cidx], qc, NT, preferred_element_type=F32))
        t0 = _topk_desc(st[0], k)
        t1 = _topk_desc(st[1], k)
        t1m = jnp.concatenate(t1, axis=0)
        cands = [a + t1m for a in t0]
        short = [cands[r][0:k // (r + 1), :] for r in range(k)]
        n_short = sum(k // (r + 1) for r in range(k))
        short.append(jnp.full((-n_short % 8, t1m.shape[1]), -jnp.inf, F32))
        best = _topk_desc(jnp.concatenate(short, axis=0), k)
        z = sum(jnp.exp(b - best[0]) for b in best)
        thr_i = jnp.full(st[0].shape, jnp.inf, F32)
        for r in range(k):
            lim = jnp.min(jnp.where(cands[r] >= best[k - 1], t1m, jnp.inf), axis=0, keepdims=True)
            thr_i = jnp.where(st[0] == t0[r], lim, thr_i)
        thr_ref[h] = thr_i
        e0_ref[h] = jnp.exp(st[0] - t0[0])
        s1_ref[h] = st[1]
        e1_ref[h] = jnp.exp(st[1] - t1[0]) / z


def _route(cfg, hn2, wq, keys):
    n, d = hn2.shape
    tm = 256
    nh, nk = cfg.peer_heads, cfg.n_keys
    qd = nh * cfg.peer_key_dim
    assert nk == LANES and cfg.peer_key_dim // 2 == LANES
    big = pl.BlockSpec((nh, nk, tm), lambda i: (0, 0, i))
    shp = jax.ShapeDtypeStruct((nh, nk, n), F32)
    return pl.pallas_call(
        functools.partial(_route_kernel, cfg),
        grid=(n // tm,),
        in_specs=[
            pl.BlockSpec((tm, d), lambda i: (i, 0)),
            pl.BlockSpec((d, qd), lambda i: (0, 0)),
            pl.BlockSpec((2, nk, cfg.peer_key_dim // 2), lambda i: (0, 0, 0)),
        ],
        out_specs=[big, big, big, big],
        out_shape=[shp, shp, shp, shp],
        compiler_params=_params("parallel"),
        name="peer_route",
    )(hn2, wq, keys)


def _gelu(x):
    return 0.5 * x * (1.0 + lax.erf(x * (2.0 ** -0.5)))


def _gate_tile(cfg, i, lanes, thr_ref, e0_ref, s1_ref, e1_ref):
    w = None
    for h in range(cfg.peer_heads):
        sel = s1_ref[h, :, lanes] >= thr_ref[h, i:i + 1, lanes]
        term = jnp.where(sel, e1_ref[h, :, lanes], 0.0) * e0_ref[h, i:i + 1, lanes]
        w = term if w is None else w + term
    return w


MXU_ROWS = 256


def _peer_kernel(cfg, ic, n_e, xt_ref, u_ref, vt_ref, thr_ref, e0_ref, s1_ref, e1_ref, o_ref, a_ref, hb_ref):
    nk = cfg.n_keys
    d, tm = o_ref.shape
    ec = ic * nk
    s = pl.program_id(0)
    e_down = jnp.maximum(s - 2, 0) % n_e
    slot = s % 2

    @pl.when(s == 0)
    def _():
        a_ref[1] = jnp.zeros(a_ref.shape[1:], F32)
        hb_ref[...] = jnp.zeros(hb_ref.shape, BF16)

    @pl.when(e_down == 0)
    def _():
        o_ref[...] = jnp.zeros(o_ref.shape, F32)

    def up(m0):
        rows = slice(m0, m0 + MXU_ROWS)
        a_ref[slot, rows, :] = jnp.dot(u_ref[rows, :], xt_ref[...], preferred_element_type=F32)

    def down(m0):
        rows = slice(m0, m0 + MXU_ROWS)
        o_ref[rows, :] += jnp.dot(vt_ref[rows, :], hb_ref[slot], preferred_element_type=F32)

    def gate(i, c):
        rows, lanes = slice(i * nk, (i + 1) * nk), slice(c * LANES, (c + 1) * LANES)
        w = _gate_tile(cfg, i, lanes, thr_ref, e0_ref, s1_ref, e1_ref)
        hb_ref[1 - slot, rows, lanes] = (w * _gelu(a_ref[1 - slot, rows, lanes])).astype(BF16)

    mxu = [functools.partial(up, m0) for m0 in range(0, ec, MXU_ROWS)]
    mxu += [functools.partial(down, m0) for m0 in range(0, d, MXU_ROWS)]
    vec = [functools.partial(gate, i, c) for i in range(ic) for c in range(tm // LANES)]
    done = 0
    for k, piece in enumerate(mxu):
        piece()
        upto = (k + 1) * len(vec) // len(mxu)
        for tile in vec[done:upto]:
            tile()
        done = upto


def _peer(cfg, hn2t, u_b, vt_b, routes):
    d, n = hn2t.shape
    nh, nk = cfg.peer_heads, cfg.n_keys
    tm = _pick(n, (512, 256))
    ic = 8
    ec = ic * nk
    n_e = nk // ic
    n_steps = (n // tm) * n_e
    assert ec % MXU_ROWS == 0 and d % MXU_ROWS == 0
    back = lambda s, k: jnp.clip(s - k, 0, n_steps - 1)
    chunked = pl.BlockSpec((nh, ic, tm), lambda s: (0, back(s, 1) % n_e, back(s, 1) // n_e))
    full = pl.BlockSpec((nh, nk, tm), lambda s: (0, 0, back(s, 1) // n_e))
    return pl.pallas_call(
        functools.partial(_peer_kernel, cfg, ic, n_e),
        grid=(n_steps + 2,),
        in_specs=[
            pl.BlockSpec((d, tm), lambda s: (0, back(s, 0) // n_e)),
            pl.BlockSpec((ec, d), lambda s: (back(s, 0) % n_e, 0)),
            pl.BlockSpec((d, ec), lambda s: (0, back(s, 2) % n_e)),
            chunked, chunked, full, full,
        ],
        out_specs=pl.BlockSpec((d, tm), lambda s: (0, back(s, 2) // n_e)),
        out_shape=jax.ShapeDtypeStruct((d, n), F32),
        scratch_shapes=[pltpu.VMEM((2, ec, tm), F32), pltpu.VMEM((2, ec, tm), BF16)],
        compiler_params=_params("arbitrary"),
        name="peer_experts",
    )(hn2t, u_b, vt_b, *routes)


def _final_kernel(eps, h2_ref, pt_ref, nw_ref, o_ref):
    h = h2_ref[...] + pt_ref[...].T
    ms = jnp.mean(h * h, axis=-1, keepdims=True)
    o_ref[...] = h * lax.rsqrt(ms + eps) * nw_ref[...]


def _final(cfg, h2, peer_t, norm_w):
    n, d = h2.shape
    tm = 256
    return pl.pallas_call(
        functools.partial(_final_kernel, cfg.eps),
        grid=(n // tm,),
        in_specs=[
            pl.BlockSpec((tm, d), lambda i: (i, 0)),
            pl.BlockSpec((d, tm), lambda i: (0, i)),
            pl.BlockSpec((1, d), lambda i: (0, 0)),
        ],
        out_specs=pl.BlockSpec((tm, d), lambda i: (i, 0)),
        out_shape=jax.ShapeDtypeStruct((n, d), F32),
        compiler_params=_params("parallel"),
        name="final_norm",
    )(h2, peer_t, norm_w.reshape(1, d))


def _layer(cfg, x, meta_tokens, norm_mix_w, w_in, conv_w, conv_b, dt_bias, a_log, d_skip, ssm_norm_w,
           w_branch_attn, w_branch_ssm, w_out, norm_ffn_w, peer_w_query, peer_sub_keys, peer_u, peer_v,
           norm_final_w):
    bsz, seq, d = x.shape
    ch = cfg.chunk
    assert seq % ch == 0 and norm_mix_w.shape[0] == 1
    n_rows = ch + seq
    topk = min(cfg.topk_max, seq // 4)
    w = w_in[0]
    o_iq = cfg.qw + 2 * cfg.kvw
    o_ik = o_iq + cfg.iw
    o_iw = o_ik + cfg.idx_dim
    o_z = o_iw + cfg.idx_heads
    o_dt = o_z + 2 * cfg.d_inner + 2 * cfg.bcw
    o_g = o_dt + cfg.ssm_heads
    w_main = jnp.concatenate([w[:, :o_ik], w[:, o_z:o_dt], w[:, o_g:]], axis=1).astype(BF16)
    zpad = lambda n: jnp.zeros((d, n), w.dtype)
    w_tail = jnp.concatenate([
        w[:, o_ik:o_z], zpad(TAIL_DT - cfg.idx_dim - cfg.idx_heads),
        w[:, o_dt:o_g], zpad(TAIL_W - TAIL_DT - cfg.ssm_heads)], axis=1).astype(BF16)

    hn = _prenorm(cfg, x, meta_tokens, norm_mix_w[0])
    main, tail = _inproj(cfg, hn.reshape(bsz * n_rows, d), w_main, w_tail)
    qk_r, qi_r, ki2, vt, tail_t = _rope(cfg, main, tail, _rope_tables(cfg, n_rows), bsz, n_rows)
    attn = _dsa(cfg, topk, qk_r, qi_r, ki2, vt, tail_t, bsz, n_rows)
    dt_raw = tail.reshape(bsz, n_rows, TAIL_W)[:, :, TAIL_DT:TAIL_DT + cfg.ssm_heads]
    ssm = _ssd(cfg, main, dt_raw, tail_t, conv_w[0], conv_b[0], dt_bias[0], a_log[0], d_skip[0], ssm_norm_w[0], bsz, n_rows)
    merged = _merge(cfg, attn, ssm, main, w_branch_attn[0].astype(BF16), w_branch_ssm[0].astype(BF16), bsz, n_rows)
    h2, hn2, hn2t = _outproj(cfg, merged, x, w_out[0].astype(BF16), norm_ffn_w[0], bsz, n_rows)
    h2 = h2.reshape(bsz * seq, d)
    hn2 = hn2.reshape(bsz * seq, d)
    routes = _route(cfg, hn2, peer_w_query[0].astype(BF16), peer_sub_keys[0].astype(BF16))
    peer_t = _peer(cfg, hn2t, peer_u[0].astype(BF16), peer_v[0].T.astype(BF16), routes)
    out = _final(cfg, h2, peer_t, norm_final_w)
    return out.reshape(bsz, seq, d)


def kernel(x, meta_tokens, norm_mix_w, w_in, conv_w, conv_b, dt_bias, a_log, d_skip, ssm_norm_w, w_branch_attn,
           w_branch_ssm, w_out, norm_ffn_w, peer_w_query, peer_sub_keys, peer_u, peer_v, norm_final_w):
    return _layer(Cfg(), x, meta_tokens, norm_mix_w, w_in, conv_w, conv_b, dt_bias, a_log, d_skip, ssm_norm_w,
                  w_branch_attn, w_branch_ssm, w_out, norm_ffn_w, peer_w_query, peer_sub_keys, peer_u, peer_v,
                  norm_final_w)
```

```python
import functools
import math
from typing import NamedTuple

import jax
import jax.numpy as jnp
from jax import lax
from jax.experimental import pallas as pl
from jax.experimental.pallas import tpu as pltpu

F32 = jnp.float32
BF16 = jnp.bfloat16
LANES = 128
LOG2E = math.log2(math.e)
NEG_BIG = -1e30
VMEM_LIMIT = 56 * 1024 * 1024


class Cfg(NamedTuple):
    d_model: int = 2048
    n_meta: int = 16
    attn_heads: int = 16
    kv_heads: int = 4
    head_dim: int = 128
    idx_heads: int = 16
    idx_dim: int = 64
    topk_max: int = 256
    rope_theta: float = 10000.0
    d_inner: int = 4096
    ssm_head_dim: int = 64
    ssm_groups: int = 8
    d_state: int = 128
    conv_width: int = 4
    chunk: int = 256
    peer_heads: int = 8
    n_keys: int = 128
    peer_key_dim: int = 256
    peer_topk: int = 16
    eps: float = 1e-6

    @property
    def lead(self):
        return self.chunk - self.n_meta

    @property
    def qw(self):
        return self.attn_heads * self.head_dim

    @property
    def kvw(self):
        return self.kv_heads * self.head_dim

    @property
    def iw(self):
        return self.idx_heads * self.idx_dim

    @property
    def ssm_heads(self):
        return self.d_inner // self.ssm_head_dim

    @property
    def hpg(self):
        return self.ssm_heads // self.ssm_groups

    @property
    def gw(self):
        return self.d_inner // self.ssm_groups

    @property
    def bcw(self):
        return self.ssm_groups * self.d_state


NT = (((1,), (1,)), ((), ()))
TAIL_W = 256
TAIL_DT = 128


def _params(*sem):
    return pltpu.CompilerParams(dimension_semantics=sem, vmem_limit_bytes=VMEM_LIMIT)


def _rms(x, w, eps):
    ms = jnp.mean(x * x, axis=-1, keepdims=True)
    return x * lax.rsqrt(ms + eps) * w


def _prenorm_kernel(cfg, x_ref, meta_ref, nw_ref, o_ref):
    c = pl.program_id(1)

    @pl.when(c == 0)
    def _():
        o_ref[0:cfg.lead, :] = jnp.zeros((cfg.lead, o_ref.shape[1]), o_ref.dtype)
        o_ref[cfg.lead:cfg.chunk, :] = _rms(meta_ref[...], nw_ref[...], cfg.eps).astype(o_ref.dtype)

    @pl.when(c > 0)
    def _():
        o_ref[...] = _rms(x_ref[...], nw_ref[...], cfg.eps).astype(o_ref.dtype)


def _prenorm(cfg, x, meta_tokens, norm_w):
    bsz, seq, d = x.shape
    ch = cfg.chunk
    nc = seq // ch + 1
    return pl.pallas_call(
        functools.partial(_prenorm_kernel, cfg),
        grid=(bsz, nc),
        in_specs=[
            pl.BlockSpec((None, ch, d), lambda b, c: (b, jnp.maximum(c - 1, 0), 0)),
            pl.BlockSpec((cfg.n_meta, d), lambda b, c: (0, 0)),
            pl.BlockSpec((1, d), lambda b, c: (0, 0)),
        ],
        out_specs=pl.BlockSpec((None, ch, d), lambda b, c: (b, c, 0)),
        out_shape=jax.ShapeDtypeStruct((bsz, nc * ch, d), BF16),
        compiler_params=_params("parallel", "parallel"),
        name="prenorm",
    )(x, meta_tokens.astype(x.dtype), norm_w.reshape(1, d))


def _inproj_kernel(hn_ref, w_ref, wt_ref, main_ref, tail_ref):
    @pl.when(pl.program_id(1) == 0)
    def _():
        tail_ref[...] = lax.dot_general(hn_ref[...], wt_ref[...], NT, preferred_element_type=F32)

    w = w_ref[...].astype(BF16)
    main_ref[...] = lax.dot_general(hn_ref[...], w, NT, preferred_element_type=F32).astype(main_ref.dtype)


def _pick(n, cands):
    for c in cands:
        if n % c == 0:
            return c
    raise ValueError(f"no tile for {n}")


def _inproj(cfg, hn2d, w_t, w_tail_t, segments):
    m, d = hn2d.shape
    nm = sum(n for _, n in segments)
    tm = _pick(m, (1088, 512, 256))
    tn = _pick(math.gcd(*[n for _, n in segments]), (1024, 512, 256))

    sub = 8
    assert all(start % sub == 0 and n % tn == 0 for start, n in segments)

    def w_row(j):
        row = j * (tn // sub)
        done = 0
        for start, n in segments:
            shift = (start - done) // sub
            row = row + jnp.where(j >= done // tn, shift, 0) - jnp.where(j >= (done + n) // tn, shift, 0)
            done += n
        return row * sub

    return pl.pallas_call(
        _inproj_kernel,
        grid=(m // tm, nm // tn),
        in_specs=[
            pl.BlockSpec((tm, d), lambda i, j: (i, 0)),
            pl.BlockSpec((pl.Element(tn), pl.Element(d)), lambda i, j: (w_row(j), 0)),
            pl.BlockSpec((TAIL_W, d), lambda i, j: (0, 0)),
        ],
        out_specs=[
            pl.BlockSpec((tm, tn), lambda i, j: (i, j)),
            pl.BlockSpec((tm, TAIL_W), lambda i, j: (i, 0)),
        ],
        out_shape=[
            jax.ShapeDtypeStruct((m, nm), BF16),
            jax.ShapeDtypeStruct((m, TAIL_W), F32),
        ],
        compiler_params=_params("parallel", "arbitrary"),
        name="inproj",
    )(hn2d, w_t, w_tail_t)


def _rope_kernel(n_q, n_qk, n_ipair, idx_dim, q_scale, qk_ref, v_ref, qi_ref, tail_ref, cs_ref, sn_ref, csi_ref,
                 sni_ref, qk_out, qi_out, ki_out, vt_out, tt_out):
    cs = cs_ref[...]
    sn = sn_ref[...]
    for hd in range(n_qk):
        sl = slice(hd * LANES, (hd + 1) * LANES)
        x = qk_ref[:, sl].astype(F32)
        y = x * cs + pltpu.roll(x, LANES // 2, 1) * sn
        if hd < n_q:
            y = y * q_scale
        qk_out[:, sl] = y.astype(BF16)
    csi = csi_ref[...]
    sni = sni_ref[...]
    half = idx_dim // 2
    lane = lax.broadcasted_iota(jnp.int32, csi.shape, 1)
    first = (lane % idx_dim) < half

    def rope_idx(x):
        rot = jnp.where(first, pltpu.roll(x, LANES - half, 1), pltpu.roll(x, half, 1))
        return x * csi + rot * sni

    for p in range(n_ipair):
        sl = slice(p * LANES, (p + 1) * LANES)
        qi_out[:, sl] = rope_idx(qi_ref[:, sl].astype(F32)).astype(BF16)
    kt = rope_idx(tail_ref[:, 0:LANES])
    klo = jnp.where(lane < idx_dim, kt, 0.0)
    ki_out[:, 0:LANES] = klo.astype(BF16)
    ki_out[:, LANES:2 * LANES] = pltpu.roll(klo, idx_dim, 1).astype(BF16)
    vt_out[...] = v_ref[...].astype(F32).T.astype(BF16)
    tt_out[...] = tail_ref[...].T


def _rope(cfg, main, tail, tabs, bsz, n_rows):
    m = main.shape[0]
    tm = 256
    nb = n_rows // tm
    n_q, n_kv = cfg.attn_heads, cfg.kv_heads
    qkw = cfg.qw + cfg.kvw
    assert cfg.head_dim == LANES and 2 * cfg.idx_dim == LANES
    qi_off = cfg.qw + 2 * cfg.kvw
    assert qi_off % cfg.iw == 0 and qkw % cfg.kvw == 0
    cs, sn, csi, sni = tabs
    tab_spec = pl.BlockSpec((tm, LANES), lambda i: (i % nb, 0))
    return pl.pallas_call(
        functools.partial(_rope_kernel, n_q, n_q + n_kv, cfg.iw // LANES, cfg.idx_dim, cfg.head_dim ** -0.5 * LOG2E),
        grid=(m // tm,),
        in_specs=[
            pl.BlockSpec((tm, qkw), lambda i: (i, 0)),
            pl.BlockSpec((tm, cfg.kvw), lambda i: (i, qkw // cfg.kvw)),
            pl.BlockSpec((tm, cfg.iw), lambda i: (i, qi_off // cfg.iw)),
            pl.BlockSpec((tm, TAIL_W), lambda i: (i, 0)),
            tab_spec, tab_spec, tab_spec, tab_spec,
        ],
        out_specs=[
            pl.BlockSpec((tm, qkw), lambda i: (i, 0)),
            pl.BlockSpec((tm, cfg.iw), lambda i: (i, 0)),
            pl.BlockSpec((tm, 2 * LANES), lambda i: (i, 0)),
            pl.BlockSpec((None, cfg.kvw, tm), lambda i: (i // nb, 0, i % nb)),
            pl.BlockSpec((None, TAIL_W, tm), lambda i: (i // nb, 0, i % nb)),
        ],
        out_shape=[
            jax.ShapeDtypeStruct((m, qkw), BF16),
            jax.ShapeDtypeStruct((m, cfg.iw), BF16),
            jax.ShapeDtypeStruct((m, 2 * LANES), BF16),
            jax.ShapeDtypeStruct((bsz, cfg.kvw, n_rows), BF16),
            jax.ShapeDtypeStruct((bsz, TAIL_W, n_rows), F32),
        ],
        compiler_params=_params("parallel"),
        name="rope",
    )(main, main, main, tail, cs, sn, csi, sni)


def _rope_tables(cfg, n_rows):
    pos = (jnp.arange(n_rows) - cfg.lead).astype(F32)

    def tab(dim):
        half = dim // 2
        inv = cfg.rope_theta ** (-jnp.arange(half, dtype=F32) / half)
        ang = pos[:, None] * inv[None, :]
        c, s = jnp.cos(ang), jnp.sin(ang)
        reps = LANES // dim
        return jnp.tile(jnp.concatenate([c, c], -1), (1, reps)), jnp.tile(jnp.concatenate([-s, s], -1), (1, reps))

    cs, sn = tab(cfg.head_dim)
    csi, sni = tab(cfg.idx_dim)
    return cs, sn, csi, sni


NEG_INF_KEY = -2139095041
CNT_CHAINS = 4


def _dsa_kernel(cfg, topk, qi_ref, wt_ref, ki_ref, q_ref, k_ref, vt_ref, o_ref, key_ref, qs_ref, acc_ref, m_ref, l_ref,
                al_ref, s_ref, p_ref):
    ch = cfg.chunk
    c = pl.program_id(1) + 1
    n_kc = c + 1
    rep = cfg.attn_heads // cfg.kv_heads
    hd = cfg.head_dim
    row0 = c * ch
    int_min = jnp.int32(-2 ** 31)
    sub8 = ch // 8

    q_row = row0 + lax.broadcasted_iota(jnp.int32, (ch, ch), 1)
    k_sub = lax.broadcasted_iota(jnp.int32, (ch, ch), 0)

    def admissible(koff):
        k_row = koff + k_sub
        return (k_row >= cfg.lead) & (k_row <= q_row)

    w_all = wt_ref[...] * (cfg.idx_heads ** -0.5 * cfg.idx_dim ** -0.5)

    def score_chunk(kc, carry):
        koff = pl.multiple_of(kc * ch, ch)
        k_lo = ki_ref[pl.ds(koff, ch), 0:LANES]
        k_hi = ki_ref[pl.ds(koff, ch), LANES:2 * LANES]
        acc = jnp.zeros((ch, ch), F32)
        for p in range(cfg.idx_heads // 2):
            qp = qi_ref[:, p * LANES:(p + 1) * LANES]
            for half, kk in ((0, k_lo), (1, k_hi)):
                h = 2 * p + half
                lg = lax.dot_general(kk, qp, NT, preferred_element_type=F32)
                acc = acc + jnp.maximum(lg, 0.0) * w_all[h:h + 1, :]
        sc = jnp.where(admissible(koff), acc, -jnp.inf)
        bits = lax.bitcast_convert_type(sc, jnp.int32)
        key_ref[pl.ds(koff, ch), :] = bits ^ ((bits >> 31) & jnp.int32(0x7FFFFFFF))
        return carry

    lax.fori_loop(0, n_kc, score_chunk, 0)

    key_ref[pl.ds(pl.multiple_of(n_kc * ch, ch), ch), :] = jnp.full((ch, ch), NEG_INF_KEY, jnp.int32)

    def count(pred_of_keys):
        def cnt_chunk(kc2, cnt):
            koff = pl.multiple_of(kc2 * (2 * ch), 2 * ch)
            ind = jnp.where(pred_of_keys(key_ref[pl.ds(koff, 2 * ch), :]), 1.0, 0.0)
            return cnt + jnp.sum(ind.reshape(2 * sub8 // CNT_CHAINS, CNT_CHAINS, 8, ch), axis=0)

        cnt = lax.fori_loop(0, (n_kc + 1) // 2, cnt_chunk, jnp.zeros((CNT_CHAINS, 8, ch), F32))
        return jnp.sum(jnp.sum(cnt, axis=0), axis=0, keepdims=True)

    kf = float(topk)

    def bit_step(it, state):
        tau_u, n_ge = state
        cand_u = tau_u | lax.shift_left(jnp.int32(1), jnp.int32(31) - it)
        cand = cand_u ^ int_min
        tot = count(lambda kv: kv >= cand)
        take = tot >= kf
        return jnp.where(take, cand_u, tau_u), jnp.where(take, tot, n_ge)

    tau_u, n_ge = lax.fori_loop(0, 32, bit_step, (jnp.zeros((1, ch), jnp.int32), jnp.full((1, ch), 2.0 * kf, F32)))
    tau = tau_u ^ int_min

    tied = jnp.max(jnp.where((n_ge > kf) & (tau != NEG_INF_KEY), 1.0, 0.0)) > 0.0

    @pl.when(jnp.logical_not(tied))
    def _():
        def bias_chunk(kc, carry):
            koff = pl.multiple_of(kc * ch, ch)
            sel = (key_ref[pl.ds(koff, ch), :] >= tau) & admissible(koff)
            key_ref[pl.ds(koff, ch), :] = lax.bitcast_convert_type(jnp.where(sel, 0.0, NEG_BIG).astype(F32), jnp.int32)
            return carry

        lax.fori_loop(0, n_kc, bias_chunk, 0)

    @pl.when(tied)
    def _():
        need = kf - count(lambda kv: kv > tau)
        below = (k_sub >= q_row - row0).astype(BF16)

        def bias_chunk(kc, seen):
            koff = pl.multiple_of(kc * ch, ch)
            kv = key_ref[pl.ds(koff, ch), :]
            eq = kv == tau
            rank = seen + jnp.dot(below, jnp.where(eq, 1.0, 0.0).astype(BF16), preferred_element_type=F32)
            sel = ((kv > tau) | (eq & (rank <= need))) & admissible(koff)
            key_ref[pl.ds(koff, ch), :] = lax.bitcast_convert_type(jnp.where(sel, 0.0, NEG_BIG).astype(F32), jnp.int32)
            return rank[ch - 1:ch, :]

        lax.fori_loop(0, n_kc, bias_chunk, jnp.zeros((1, ch), F32))

    for g in range(cfg.kv_heads):
        for r in range(rep):
            h = g * rep + r
            qs_ref[g, r * ch:(r + 1) * ch, :] = q_ref[:, h * hd:(h + 1) * hd]
    acc_ref[...] = jnp.zeros(acc_ref.shape, F32)
    m_ref[...] = jnp.full(m_ref.shape, NEG_BIG, F32)
    l_ref[...] = jnp.zeros(l_ref.shape, F32)

    def attn_chunk(kc, carry):
        koff = pl.multiple_of(kc * ch, ch)

        def scores(g):
            kg = k_ref[pl.ds(koff, ch), g * hd:(g + 1) * hd]
            s_ref[g % 2] = lax.dot_general(kg, qs_ref[g], NT, preferred_element_type=F32)

        def softmax_pv(g):
            for t in range(rep * ch // LANES):
                lanes = slice(t * LANES, (t + 1) * LANES)
                qlanes = slice((t * LANES) % ch, (t * LANES) % ch + LANES)
                s = s_ref[g % 2, :, lanes] + lax.bitcast_convert_type(key_ref[pl.ds(koff, ch), qlanes], F32)
                m_old = m_ref[g, 0:1, lanes]
                m_new = jnp.maximum(m_old, jnp.max(s, axis=0, keepdims=True))
                alpha = jnp.exp2(m_old - m_new)
                p = jnp.exp2(s - m_new)
                l_ref[g, 0:1, lanes] = alpha * l_ref[g, 0:1, lanes] + jnp.sum(p, axis=0, keepdims=True)
                m_ref[g, 0:1, lanes] = m_new
                al_ref[g % 2, 0:1, lanes] = alpha
                p_ref[g % 2, :, lanes] = p.astype(BF16)
            vtg = vt_ref[g * hd:(g + 1) * hd, pl.ds(koff, ch)]
            acc_ref[g] = al_ref[g % 2, 0:1, :] * acc_ref[g] + jnp.dot(vtg, p_ref[g % 2], preferred_element_type=F32)

        scores(0)
        for g in range(cfg.kv_heads):
            if g + 1 < cfg.kv_heads:
                scores(g + 1)
            softmax_pv(g)
        return carry

    lax.fori_loop(0, n_kc, attn_chunk, 0)

    for g in range(cfg.kv_heads):
        og = acc_ref[g] / l_ref[g, 0:1, :]
        for r in range(rep):
            h = g * rep + r
            o_ref[:, h * hd:(h + 1) * hd] = og[:, r * ch:(r + 1) * ch].T.astype(o_ref.dtype)


def _dsa(cfg, topk, qk_r, qi_r, ki2, vt, tail_t, bsz, n_rows):
    ch = cfg.chunk
    nq = n_rows // ch - 1
    rep = cfg.attn_heads // cfg.kv_heads
    qk3 = qk_r.reshape(bsz, n_rows, -1)
    qi3 = qi_r.reshape(bsz, n_rows, -1)
    ki3 = ki2.reshape(bsz, n_rows, -1)
    assert cfg.qw % cfg.kvw == 0 and cfg.idx_dim % cfg.idx_heads == 0 and cfg.idx_heads % 8 == 0
    k_blk = cfg.qw // cfg.kvw
    return pl.pallas_call(
        functools.partial(_dsa_kernel, cfg, topk),
        grid=(bsz, nq),
        in_specs=[
            pl.BlockSpec((None, ch, cfg.iw), lambda b, i: (b, i + 1, 0)),
            pl.BlockSpec((None, cfg.idx_heads, ch), lambda b, i: (b, cfg.idx_dim // cfg.idx_heads, i + 1)),
            pl.BlockSpec((None, n_rows, 2 * LANES), lambda b, i: (b, 0, 0)),
            pl.BlockSpec((None, ch, cfg.qw), lambda b, i: (b, i + 1, 0)),
            pl.BlockSpec((None, n_rows, cfg.kvw), lambda b, i: (b, 0, k_blk)),
            pl.BlockSpec((None, cfg.kvw, n_rows), lambda b, i: (b, 0, 0)),
        ],
        out_specs=pl.BlockSpec((None, ch, cfg.qw), lambda b, i: (b, i, 0)),
        out_shape=jax.ShapeDtypeStruct((bsz, nq * ch, cfg.qw), BF16),
        scratch_shapes=[
            pltpu.VMEM((n_rows + ch, ch), jnp.int32),
            pltpu.VMEM((cfg.kv_heads, rep * ch, cfg.head_dim), BF16),
            pltpu.VMEM((cfg.kv_heads, cfg.head_dim, rep * ch), F32),
            pltpu.VMEM((cfg.kv_heads, 8, rep * ch), F32),
            pltpu.VMEM((cfg.kv_heads, 8, rep * ch), F32),
            pltpu.VMEM((2, 8, rep * ch), F32),
            pltpu.VMEM((2, ch, rep * ch), F32),
            pltpu.VMEM((2, ch, rep * ch), BF16),
        ],
        compiler_params=_params("parallel", "arbitrary"),
        name="dsa",
    )(qi3, tail_t, ki3, qk3, qk3, vt)


def _split3(a):
    hi = a.astype(BF16)
    r1 = a - hi.astype(F32)
    mid = r1.astype(BF16)
    lo = (r1 - mid.astype(F32)).astype(BF16)
    return hi, mid, lo


def _dot3_nt(ones_lhs, a):
    return sum(lax.dot_general(ones_lhs, t, NT, preferred_element_type=F32) for t in _split3(a))


def _dot3_lhs(a, ones_rhs):
    return sum(jnp.dot(t, ones_rhs, preferred_element_type=F32) for t in _split3(a))


def _silu(x):
    return x * jax.nn.sigmoid(x)


def _ssd_kernel(cfg, z_ref, x_ref, b_ref, c_ref, dtr_ref, cwx_ref, cwb_ref, cwc_ref, cbx_ref, cbb_ref, cbc_ref,
                dbr_ref, alr_ref, dsk_ref, nw_ref, y_ref, state_ref, carry_ref, head_ref, shift_ref, tri_ref):
    ch, gw, ns, hdim, hpg = cfg.chunk, cfg.gw, cfg.d_state, cfg.ssm_head_dim, cfg.hpg
    cwid = cfg.conv_width
    c = pl.program_id(1)
    g = pl.program_id(2)
    ew = gw + 2 * ns

    @pl.when(c == 0)
    def _():
        state_ref[g] = jnp.zeros((ns, gw), F32)
        carry_ref[g] = jnp.zeros((8, ew), F32)

    @pl.when((c == 0) & (g == 0))
    def _():
        ti = lax.broadcasted_iota(jnp.int32, (ch, ch), 0)
        sj = lax.broadcasted_iota(jnp.int32, (ch, ch), 1)
        for k in range(1, cwid):
            shift_ref[k - 1] = jnp.where(sj == ti - k, 1.0, 0.0).astype(BF16)
        tri_ref[0] = jnp.where(ti >= sj, 1.0, 0.0).astype(BF16)
        tri_ref[1] = jnp.where(ti <= sj, 1.0, 0.0).astype(BF16)

    cur = jnp.concatenate([x_ref[...], b_ref[...], c_ref[...]], axis=1)
    curf = cur.astype(F32)
    head_ref[0:8, :] = carry_ref[g]
    head_ref[8:16, :] = curf[0:8, :]
    carry_ref[g] = curf[ch - 8:ch, :]
    cw_all = jnp.concatenate([cwx_ref[...], cwb_ref[...], cwc_ref[...]], axis=1)
    cb_all = jnp.concatenate([cbx_ref[...], cbb_ref[...], cbc_ref[...]], axis=1)
    acc = curf * cw_all[cwid - 1:cwid, :] + cb_all
    for k in range(1, cwid):
        sh = jnp.dot(shift_ref[k - 1], cur, preferred_element_type=F32)
        sh = jnp.concatenate([head_ref[8 - k:16 - k, :], sh[8:, :]], axis=0)
        acc = acc + sh * cw_all[cwid - 1 - k:cwid - k, :]
    conv = _silu(acc)
    xc = conv[:, 0:gw]
    bc = conv[:, gw:gw + ns]
    cc = conv[:, gw + ns:ew]

    coli = lax.broadcasted_iota(jnp.int32, (hpg, ch), 1)
    is_pad_r = (coli < cfg.lead) & (c == 0)
    dt_r = jnp.where(is_pad_r, 0.0, jax.nn.softplus(dtr_ref[...] + dbr_ref[...]))
    a_r = dt_r * (-jnp.exp(alr_ref[...]) * LOG2E)
    li = lax.broadcasted_iota(jnp.int32, (ch, ch), 0)
    si = lax.broadcasted_iota(jnp.int32, (ch, ch), 1)
    causal = li >= si
    acs_c = _dot3_nt(tri_ref[0], a_r)
    acs_r = _dot3_lhs(a_r, tri_ref[1])
    a_end = acs_r[:, ch - 1:ch]
    wdec_r = dt_r * jnp.exp2(a_end - acs_r)

    cb = lax.dot_general(cc.astype(BF16), bc.astype(BF16), NT, preferred_element_type=F32)
    cb = jnp.where(causal, cb, 0.0)
    bt = bc.T
    xb = xc.astype(BF16)
    prev = state_ref[g]
    prev_b = prev.astype(BF16)
    lane_p = lax.broadcasted_iota(jnp.int32, (ch, 2 * hdim), 1)
    lane_s = lax.broadcasted_iota(jnp.int32, (ns, 2 * hdim), 1)

    y_pairs = []
    st_pairs = []
    for pr in range(hpg // 2):
        psl = slice(pr * 2 * hdim, (pr + 1) * 2 * hdim)
        x_pair = xb[:, psl]
        prev_pair = prev_b[:, psl]
        ys, sts = [], []
        for half in range(2):
            j = 2 * pr + half
            col = acs_c[:, j:j + 1]
            row = acs_r[j:j + 1, :]
            lmat = jnp.exp2(jnp.minimum(col - row, 0.0))
            mmat = (cb * lmat * dt_r[j:j + 1, :]).astype(BF16)
            ce = (cc * jnp.exp2(col)).astype(BF16)
            ys.append(jnp.dot(mmat, x_pair, preferred_element_type=F32)
                      + jnp.dot(ce, prev_pair, preferred_element_type=F32))
            btw = (bt * wdec_r[j:j + 1, :]).astype(BF16)
            sts.append(jnp.dot(btw, x_pair, preferred_element_type=F32))
        y_pairs.append(jnp.where(lane_p < hdim, ys[0], ys[1]))
        st_pairs.append(jnp.where(lane_s < hdim, sts[0], sts[1]))
    y = jnp.concatenate(y_pairs, axis=1)
    st = jnp.concatenate(st_pairs, axis=1)

    lane_g = lax.broadcasted_iota(jnp.int32, (1, gw), 1) // hdim
    cdec = jnp.zeros((1, gw), F32)
    for j in range(hpg):
        cdec = jnp.where(lane_g == j, jnp.exp2(a_end[j:j + 1, :]), cdec)
    state_ref[g] = prev * cdec + st

    y = y + xc * dsk_ref[...]
    y = y * _silu(z_ref[...].astype(F32))
    ms = jnp.mean(y * y, axis=-1, keepdims=True)
    y_ref[...] = (y * lax.rsqrt(ms + cfg.eps) * nw_ref[...]).astype(y_ref.dtype)


def _ssd(cfg, main, tail_t, conv_w, conv_b, dt_bias, a_log, d_skip, norm_w, bsz, n_rows):
    ch, gw, ns, hpg, ng = cfg.chunk, cfg.gw, cfg.d_state, cfg.hpg, cfg.ssm_groups
    nc = n_rows // ch
    di = cfg.d_inner
    main3 = main.reshape(bsz, n_rows, -1)
    z_off = cfg.qw + 2 * cfg.kvw + cfg.iw
    x_off = z_off + di
    b_off = x_off + di
    c_off = b_off + cfg.bcw
    assert z_off % gw == 0 and b_off % ns == 0 and hpg % 2 == 0 and 2 * cfg.ssm_head_dim == LANES
    cwx, cwb, cwc = conv_w[:, :di], conv_w[:, di:di + cfg.bcw], conv_w[:, di + cfg.bcw:]
    cb2 = conv_b.reshape(1, -1)
    cbx, cbb, cbc = cb2[:, :di], cb2[:, di:di + cfg.bcw], cb2[:, di + cfg.bcw:]
    ew = gw + 2 * ns
    cw = cfg.conv_width
    grp = lambda b, c, g: (b, c, g)
    return pl.pallas_call(
        functools.partial(_ssd_kernel, cfg),
        grid=(bsz, nc, ng),
        in_specs=[
            pl.BlockSpec((None, ch, gw), lambda b, c, g: (b, c, z_off // gw + g)),
            pl.BlockSpec((None, ch, gw), lambda b, c, g: (b, c, x_off // gw + g)),
            pl.BlockSpec((None, ch, ns), lambda b, c, g: (b, c, b_off // ns + g)),
            pl.BlockSpec((None, ch, ns), lambda b, c, g: (b, c, c_off // ns + g)),
            pl.BlockSpec((None, hpg, ch), lambda b, c, g: (b, TAIL_DT // hpg + g, c)),
            pl.BlockSpec((cw, gw), lambda b, c, g: (0, g)),
            pl.BlockSpec((cw, ns), lambda b, c, g: (0, g)),
            pl.BlockSpec((cw, ns), lambda b, c, g: (0, g)),
            pl.BlockSpec((1, gw), lambda b, c, g: (0, g)),
            pl.BlockSpec((1, ns), lambda b, c, g: (0, g)),
            pl.BlockSpec((1, ns), lambda b, c, g: (0, g)),
            pl.BlockSpec((None, hpg, 1), lambda b, c, g: (g, 0, 0)),
            pl.BlockSpec((None, hpg, 1), lambda b, c, g: (g, 0, 0)),
            pl.BlockSpec((1, gw), lambda b, c, g: (0, g)),
            pl.BlockSpec((1, gw), lambda b, c, g: (0, g)),
        ],
        out_specs=pl.BlockSpec((None, ch, gw), grp),
        out_shape=jax.ShapeDtypeStruct((bsz, n_rows, di), BF16),
        scratch_shapes=[
            pltpu.VMEM((ng, ns, gw), F32),
            pltpu.VMEM((ng, 8, ew), F32),
            pltpu.VMEM((16, ew), F32),
            pltpu.VMEM((cw - 1, ch, ch), BF16),
            pltpu.VMEM((2, ch, ch), BF16),
        ],
        compiler_params=_params("arbitrary", "arbitrary", "arbitrary"),
        name="ssd",
    )(main3, main3, main3, main3, tail_t, cwx, cwb, cwc, cbx, cbb, cbc,
      dt_bias.reshape(ng, hpg, 1), a_log.reshape(ng, hpg, 1),
      jnp.repeat(d_skip, cfg.ssm_head_dim).reshape(1, di), norm_w.reshape(1, di))


def _merge_kernel(attn_ref, ssm_ref, ga_ref, gs_ref, wa_ref, ws_ref, o_ref):
    pa = jnp.dot(attn_ref[...], wa_ref[...], preferred_element_type=F32)
    ps = jnp.dot(ssm_ref[...], ws_ref[...], preferred_element_type=F32)
    ga = jax.nn.sigmoid(ga_ref[...].astype(F32))
    gs = jax.nn.sigmoid(gs_ref[...].astype(F32))
    o_ref[...] = (ga * pa + gs * ps).astype(o_ref.dtype)


def _merge(cfg, attn, ssm, main, w_a, w_s, bsz, n_rows):
    ch, d = cfg.chunk, cfg.d_model
    nq = n_rows // ch - 1
    tn = _pick(d, (1024, 512, 256))
    main3 = main.reshape(bsz, n_rows, -1)
    g_off = cfg.qw + 2 * cfg.kvw + cfg.iw + 2 * cfg.d_inner + 2 * cfg.bcw
    assert g_off % tn == 0
    return pl.pallas_call(
        _merge_kernel,
        grid=(d // tn, bsz, nq),
        in_specs=[
            pl.BlockSpec((None, ch, cfg.qw), lambda j, b, i: (b, i, 0)),
            pl.BlockSpec((None, ch, cfg.d_inner), lambda j, b, i: (b, i + 1, 0)),
            pl.BlockSpec((None, ch, tn), lambda j, b, i: (b, i + 1, g_off // tn + j)),
            pl.BlockSpec((None, ch, tn), lambda j, b, i: (b, i + 1, (g_off + d) // tn + j)),
            pl.BlockSpec((cfg.qw, tn), lambda j, b, i: (0, j)),
            pl.BlockSpec((cfg.d_inner, tn), lambda j, b, i: (0, j)),
        ],
        out_specs=pl.BlockSpec((None, ch, tn), lambda j, b, i: (b, i, j)),
        out_shape=jax.ShapeDtypeStruct((bsz, nq * ch, d), BF16),
        compiler_params=_params("parallel", "parallel", "parallel"),
        name="merge",
    )(attn, ssm, main3, main3, w_a, w_s)


def _outproj_kernel(eps, m_ref, h_ref, w_ref, nw_ref, h2_ref, hn_ref, hnt_ref):
    h2 = h_ref[...] + jnp.dot(m_ref[...], w_ref[...], preferred_element_type=F32)
    h2_ref[...] = h2
    ms = jnp.mean(h2 * h2, axis=-1, keepdims=True)
    hn = h2 * lax.rsqrt(ms + eps) * nw_ref[...]
    hn_ref[...] = hn.astype(hn_ref.dtype)
    hnt_ref[...] = hn.T.astype(hnt_ref.dtype)


def _outproj(cfg, merged, x, w_out, norm_w, bsz, n_rows):
    ch, d = cfg.chunk, cfg.d_model
    nq = n_rows // ch - 1
    return pl.pallas_call(
        functools.partial(_outproj_kernel, cfg.eps),
        grid=(bsz, nq),
        in_specs=[
            pl.BlockSpec((None, ch, d), lambda b, i: (b, i, 0)),
            pl.BlockSpec((None, ch, d), lambda b, i: (b, i, 0)),
            pl.BlockSpec((d, d), lambda b, i: (0, 0)),
            pl.BlockSpec((1, d), lambda b, i: (0, 0)),
        ],
        out_specs=[
            pl.BlockSpec((None, ch, d), lambda b, i: (b, i, 0)),
            pl.BlockSpec((None, ch, d), lambda b, i: (b, i, 0)),
            pl.BlockSpec((d, ch), lambda b, i: (0, b * nq + i)),
        ],
        out_shape=[
            jax.ShapeDtypeStruct((bsz, nq * ch, d), F32),
            jax.ShapeDtypeStruct((bsz, nq * ch, d), BF16),
            jax.ShapeDtypeStruct((d, bsz * nq * ch), BF16),
        ],
        compiler_params=_params("parallel", "parallel"),
        name="outproj",
    )(merged, x, w_out, norm_w.reshape(1, d))


def _topk_desc(s, k):
    out = []
    for _ in range(k):
        m = jnp.max(s, axis=0, keepdims=True)
        out.append(m)
        s = jnp.where(s >= m, -jnp.inf, s)
    return out


def _route_kernel(cfg, x_ref, wq_ref, keys_ref, thr_ref, e0_ref, s1_ref, e1_ref):
    nk, k = cfg.n_keys, cfg.peer_topk
    half = cfg.peer_key_dim // 2
    q = jnp.dot(x_ref[...], wq_ref[...], preferred_element_type=F32)
    for h in range(cfg.peer_heads):
        st = []
        for cidx in range(2):
            off = (2 * h + cidx) * half
            qc = q[:, off:off + half].astype(BF16)
            st.append(lax.dot_general(keys_ref[cidx], qc, NT, preferred_element_type=F32))
        t0 = _topk_desc(st[0], k)
        t1 = _topk_desc(st[1], k)
        t1m = jnp.concatenate(t1, axis=0)
        cands = [a + t1m for a in t0]
        short = [cands[r][0:k // (r + 1), :] for r in range(k)]
        n_short = sum(k // (r + 1) for r in range(k))
        short.append(jnp.full((-n_short % 8, t1m.shape[1]), -jnp.inf, F32))
        best = _topk_desc(jnp.concatenate(short, axis=0), k)
        z = sum(jnp.exp(b - best[0]) for b in best)
        thr_i = jnp.full(st[0].shape, jnp.inf, F32)
        for r in range(k):
            lim = jnp.min(jnp.where(cands[r] >= best[k - 1], t1m, jnp.inf), axis=0, keepdims=True)
            thr_i = jnp.where(st[0] == t0[r], lim, thr_i)
        thr_ref[h] = thr_i
        e0_ref[h] = jnp.exp(st[0] - t0[0])
        s1_ref[h] = st[1]
        e1_ref[h] = jnp.exp(st[1] - t1[0]) / z


def _route(cfg, hn2, wq, keys):
    n, d = hn2.shape
    tm = 256
    nh, nk = cfg.peer_heads, cfg.n_keys
    qd = nh * cfg.peer_key_dim
    assert nk == LANES and cfg.peer_key_dim // 2 == LANES
    big = pl.BlockSpec((nh, nk, tm), lambda i: (0, 0, i))
    shp = jax.ShapeDtypeStruct((nh, nk, n), F32)
    return pl.pallas_call(
        functools.partial(_route_kernel, cfg),
        grid=(n // tm,),
        in_specs=[
            pl.BlockSpec((tm, d), lambda i: (i, 0)),
            pl.BlockSpec((d, qd), lambda i: (0, 0)),
            pl.BlockSpec((2, nk, cfg.peer_key_dim // 2), lambda i: (0, 0, 0)),
        ],
        out_specs=[big, big, big, big],
        out_shape=[shp, shp, shp, shp],
        compiler_params=_params("parallel"),
        name="peer_route",
    )(hn2, wq, keys)


def _gelu(x):
    return 0.5 * x * (1.0 + lax.erf(x * (2.0 ** -0.5)))


def _gate_tile(cfg, i, lanes, thr_ref, e0_ref, s1_ref, e1_ref):
    w = None
    for h in range(cfg.peer_heads):
        sel = s1_ref[h, :, lanes] >= thr_ref[h, i:i + 1, lanes]
        term = jnp.where(sel, e1_ref[h, :, lanes], 0.0) * e0_ref[h, i:i + 1, lanes]
        w = term if w is None else w + term
    return w


MXU_ROWS = 256


def _peer_kernel(cfg, ic, n_e, xt_ref, u_ref, vt_ref, thr_ref, e0_ref, s1_ref, e1_ref, o_ref, a_ref, hb_ref):
    nk = cfg.n_keys
    d, tm = o_ref.shape
    ec = ic * nk
    s = pl.program_id(0)
    e_down = jnp.maximum(s - 2, 0) % n_e
    slot = s % 2

    @pl.when(s == 0)
    def _():
        a_ref[1] = jnp.zeros(a_ref.shape[1:], F32)
        hb_ref[...] = jnp.zeros(hb_ref.shape, BF16)

    @pl.when(e_down == 0)
    def _():
        o_ref[...] = jnp.zeros(o_ref.shape, F32)

    def up(m0):
        rows = slice(m0, m0 + MXU_ROWS)
        a_ref[slot, rows, :] = jnp.dot(u_ref[rows, :], xt_ref[...], preferred_element_type=F32)

    def down(m0):
        rows = slice(m0, m0 + MXU_ROWS)
        o_ref[rows, :] += jnp.dot(vt_ref[rows, :], hb_ref[slot], preferred_element_type=F32)

    def gate(i, c):
        rows, lanes = slice(i * nk, (i + 1) * nk), slice(c * LANES, (c + 1) * LANES)
        w = _gate_tile(cfg, i, lanes, thr_ref, e0_ref, s1_ref, e1_ref)
        hb_ref[1 - slot, rows, lanes] = (w * _gelu(a_ref[1 - slot, rows, lanes])).astype(BF16)

    mxu = [functools.partial(up, m0) for m0 in range(0, ec, MXU_ROWS)]
    mxu += [functools.partial(down, m0) for m0 in range(0, d, MXU_ROWS)]
    vec = [functools.partial(gate, i, c) for i in range(ic) for c in range(tm // LANES)]
    done = 0
    for k, piece in enumerate(mxu):
        piece()
        upto = (k + 1) * len(vec) // len(mxu)
        for tile in vec[done:upto]:
            tile()
        done = upto


def _peer(cfg, hn2t, u_b, vt_b, routes):
    d, n = hn2t.shape
    nh, nk = cfg.peer_heads, cfg.n_keys
    tm = _pick(n, (512, 256))
    ic = 8
    ec = ic * nk
    n_e = nk // ic
    n_steps = (n // tm) * n_e
    assert ec % MXU_ROWS == 0 and d % MXU_ROWS == 0
    back = lambda s, k: jnp.clip(s - k, 0, n_steps - 1)
    chunked = pl.BlockSpec((nh, ic, tm), lambda s: (0, back(s, 1) % n_e, back(s, 1) // n_e))
    full = pl.BlockSpec((nh, nk, tm), lambda s: (0, 0, back(s, 1) // n_e))
    return pl.pallas_call(
        functools.partial(_peer_kernel, cfg, ic, n_e),
        grid=(n_steps + 2,),
        in_specs=[
            pl.BlockSpec((d, tm), lambda s: (0, back(s, 0) // n_e)),
            pl.BlockSpec((ec, d), lambda s: (back(s, 0) % n_e, 0)),
            pl.BlockSpec((d, ec), lambda s: (0, back(s, 2) % n_e)),
            chunked, chunked, full, full,
        ],
        out_specs=pl.BlockSpec((d, tm), lambda s: (0, back(s, 2) // n_e)),
        out_shape=jax.ShapeDtypeStruct((d, n), F32),
        scratch_shapes=[pltpu.VMEM((2, ec, tm), F32), pltpu.VMEM((2, ec, tm), BF16)],
        compiler_params=_params("arbitrary"),
        name="peer_experts",
    )(hn2t, u_b, vt_b, *routes)


def _final_kernel(eps, h2_ref, pt_ref, nw_ref, o_ref):
    h = h2_ref[...] + pt_ref[...].T
    ms = jnp.mean(h * h, axis=-1, keepdims=True)
    o_ref[...] = h * lax.rsqrt(ms + eps) * nw_ref[...]


def _final(cfg, h2, peer_t, norm_w):
    n, d = h2.shape
    tm = 256
    return pl.pallas_call(
        functools.partial(_final_kernel, cfg.eps),
        grid=(n // tm,),
        in_specs=[
            pl.BlockSpec((tm, d), lambda i: (i, 0)),
            pl.BlockSpec((d, tm), lambda i: (0, i)),
            pl.BlockSpec((1, d), lambda i: (0, 0)),
        ],
        out_specs=pl.BlockSpec((tm, d), lambda i: (i, 0)),
        out_shape=jax.ShapeDtypeStruct((n, d), F32),
        compiler_params=_params("parallel"),
        name="final_norm",
    )(h2, peer_t, norm_w.reshape(1, d))


def _layer(cfg, x, meta_tokens, norm_mix_w, w_in, conv_w, conv_b, dt_bias, a_log, d_skip, ssm_norm_w,
           w_branch_attn, w_branch_ssm, w_out, norm_ffn_w, peer_w_query, peer_sub_keys, peer_u, peer_v,
           norm_final_w):
    bsz, seq, d = x.shape
    ch = cfg.chunk
    assert seq % ch == 0 and norm_mix_w.shape[0] == 1
    n_rows = ch + seq
    topk = min(cfg.topk_max, seq // 4)
    w_t = w_in[0].T
    o_iq = cfg.qw + 2 * cfg.kvw
    o_ik = o_iq + cfg.iw
    o_iw = o_ik + cfg.idx_dim
    o_z = o_iw + cfg.idx_heads
    o_dt = o_z + 2 * cfg.d_inner + 2 * cfg.bcw
    o_g = o_dt + cfg.ssm_heads
    segments = ((0, o_ik), (o_z, o_dt - o_z), (o_g, w_t.shape[0] - o_g))
    zpad = lambda n: jnp.zeros((n, d), w_t.dtype)
    w_tail_t = jnp.concatenate([
        w_t[o_ik:o_z], zpad(TAIL_DT - cfg.idx_dim - cfg.idx_heads),
        w_t[o_dt:o_g], zpad(TAIL_W - TAIL_DT - cfg.ssm_heads)], axis=0).astype(BF16)

    hn = _prenorm(cfg, x, meta_tokens, norm_mix_w[0])
    main, tail = _inproj(cfg, hn.reshape(bsz * n_rows, d), w_t, w_tail_t, segments)
    qk_r, qi_r, ki2, vt, tail_t = _rope(cfg, main, tail, _rope_tables(cfg, n_rows), bsz, n_rows)
    attn = _dsa(cfg, topk, qk_r, qi_r, ki2, vt, tail_t, bsz, n_rows)
    ssm = _ssd(cfg, main, tail_t, conv_w[0], conv_b[0], dt_bias[0], a_log[0], d_skip[0], ssm_norm_w[0], bsz, n_rows)
    merged = _merge(cfg, attn, ssm, main, w_branch_attn[0].astype(BF16), w_branch_ssm[0].astype(BF16), bsz, n_rows)
    h2, hn2, hn2t = _outproj(cfg, merged, x, w_out[0].astype(BF16), norm_ffn_w[0], bsz, n_rows)
    h2 = h2.reshape(bsz * seq, d)
    hn2 = hn2.reshape(bsz * seq, d)
    routes = _route(cfg, hn2, peer_w_query[0].astype(BF16), peer_sub_keys[0].astype(BF16))
    peer_t = _peer(cfg, hn2t, peer_u[0].astype(BF16), peer_v[0].T.astype(BF16), routes)
    out = _final(cfg, h2, peer_t, norm_final_w)
    return out.reshape(bsz, seq, d)


def kernel(x, meta_tokens, norm_mix_w, w_in, conv_w, conv_b, dt_bias, a_log, d_skip, ssm_norm_w, w_branch_attn,
           w_branch_ssm, w_out, norm_ffn_w, peer_w_query, peer_sub_keys, peer_u, peer_v, norm_final_w):
    return _layer(Cfg(), x, meta_tokens, norm_mix_w, w_in, conv_w, conv_b, dt_bias, a_log, d_skip, ssm_norm_w,
                  w_branch_attn, w_branch_ssm, w_out, norm_ffn_w, peer_w_query, peer_sub_keys, peer_u, peer_v,
                  norm_final_w)
```

```python
import functools
import math
from typing import NamedTuple

import jax
import jax.numpy as jnp
from jax import lax
from jax.experimental import pallas as pl
from jax.experimental.pallas import tpu as pltpu

F32 = jnp.float32
BF16 = jnp.bfloat16
LANES = 128
LOG2E = math.log2(math.e)
NEG_BIG = -1e30
VMEM_LIMIT = 56 * 1024 * 1024


class Cfg(NamedTuple):
    d_model: int = 2048
    n_meta: int = 16
    attn_heads: int = 16
    kv_heads: int = 4
    head_dim: int = 128
    idx_heads: int = 16
    idx_dim: int = 64
    topk_max: int = 256
    rope_theta: float = 10000.0
    d_inner: int = 4096
    ssm_head_dim: int = 64
    ssm_groups: int = 8
    d_state: int = 128
    conv_width: int = 4
    chunk: int = 256
    peer_heads: int = 8
    n_keys: int = 128
    peer_key_dim: int = 256
    peer_topk: int = 16
    eps: float = 1e-6

    @property
    def lead(self):
        return self.chunk - self.n_meta

    @property
    def qw(self):
        return self.attn_heads * self.head_dim

    @property
    def kvw(self):
        return self.kv_heads * self.head_dim

    @property
    def iw(self):
        return self.idx_heads * self.idx_dim

    @property
    def ssm_heads(self):
        return self.d_inner // self.ssm_head_dim

    @property
    def hpg(self):
        return self.ssm_heads // self.ssm_groups

    @property
    def gw(self):
        return self.d_inner // self.ssm_groups

    @property
    def bcw(self):
        return self.ssm_groups * self.d_state


NT = (((1,), (1,)), ((), ()))
TAIL_W = 256
TAIL_DT = 128


def _params(*sem):
    return pltpu.CompilerParams(dimension_semantics=sem, vmem_limit_bytes=VMEM_LIMIT)


def _rms(x, w, eps):
    ms = jnp.mean(x * x, axis=-1, keepdims=True)
    return x * lax.rsqrt(ms + eps) * w


def _prenorm_kernel(cfg, x_ref, meta_ref, nw_ref, o_ref):
    c = pl.program_id(1)

    @pl.when(c == 0)
    def _():
        o_ref[0:cfg.lead, :] = jnp.zeros((cfg.lead, o_ref.shape[1]), o_ref.dtype)
        o_ref[cfg.lead:cfg.chunk, :] = _rms(meta_ref[...], nw_ref[...], cfg.eps).astype(o_ref.dtype)

    @pl.when(c > 0)
    def _():
        o_ref[...] = _rms(x_ref[...], nw_ref[...], cfg.eps).astype(o_ref.dtype)


def _prenorm(cfg, x, meta_tokens, norm_w):
    bsz, seq, d = x.shape
    ch = cfg.chunk
    nc = seq // ch + 1
    return pl.pallas_call(
        functools.partial(_prenorm_kernel, cfg),
        grid=(bsz, nc),
        in_specs=[
            pl.BlockSpec((None, ch, d), lambda b, c: (b, jnp.maximum(c - 1, 0), 0)),
            pl.BlockSpec((cfg.n_meta, d), lambda b, c: (0, 0)),
            pl.BlockSpec((1, d), lambda b, c: (0, 0)),
        ],
        out_specs=pl.BlockSpec((None, ch, d), lambda b, c: (b, c, 0)),
        out_shape=jax.ShapeDtypeStruct((bsz, nc * ch, d), BF16),
        compiler_params=_params("parallel", "parallel"),
        name="prenorm",
    )(x, meta_tokens.astype(x.dtype), norm_w.reshape(1, d))


def _inproj_kernel(hn_ref, w_ref, wt_ref, main_ref, tail_ref):
    @pl.when(pl.program_id(1) == 0)
    def _():
        tail_ref[...] = lax.dot_general(hn_ref[...], wt_ref[...].astype(BF16), NT, preferred_element_type=F32)

    w = w_ref[...].astype(BF16)
    main_ref[...] = lax.dot_general(hn_ref[...], w, NT, preferred_element_type=F32).astype(main_ref.dtype)


def _pick(n, cands):
    for c in cands:
        if n % c == 0:
            return c
    raise ValueError(f"no tile for {n}")


def _inproj(cfg, hn2d, w_t, w_tail_t, segments):
    m, d = hn2d.shape
    nm = sum(n for _, n in segments)
    tm = _pick(m, (1088, 512, 256))
    tn = _pick(math.gcd(*[n for _, n in segments]), (1024, 512, 256))

    sub = 8
    assert all(start % sub == 0 and n % tn == 0 for start, n in segments)

    def w_row(j):
        row = j * (tn // sub)
        done = 0
        for start, n in segments:
            shift = (start - done) // sub
            row = row + jnp.where(j >= done // tn, shift, 0) - jnp.where(j >= (done + n) // tn, shift, 0)
            done += n
        return row * sub

    return pl.pallas_call(
        _inproj_kernel,
        grid=(m // tm, nm // tn),
        in_specs=[
            pl.BlockSpec((tm, d), lambda i, j: (i, 0)),
            pl.BlockSpec((pl.Element(tn), pl.Element(d)), lambda i, j: (w_row(j), 0)),
            pl.BlockSpec((TAIL_W, d), lambda i, j: (0, 0)),
        ],
        out_specs=[
            pl.BlockSpec((tm, tn), lambda i, j: (i, j)),
            pl.BlockSpec((tm, TAIL_W), lambda i, j: (i, 0)),
        ],
        out_shape=[
            jax.ShapeDtypeStruct((m, nm), BF16),
            jax.ShapeDtypeStruct((m, TAIL_W), F32),
        ],
        compiler_params=_params("parallel", "arbitrary"),
        name="inproj",
    )(hn2d, w_t, w_tail_t)


def _rope_kernel(n_q, n_qk, n_ipair, idx_dim, q_scale, qk_ref, v_ref, qi_ref, tail_ref, cs_ref, sn_ref, csi_ref,
                 sni_ref, qk_out, qi_out, ki_out, vt_out, tt_out):
    cs = cs_ref[...]
    sn = sn_ref[...]
    for hd in range(n_qk):
        sl = slice(hd * LANES, (hd + 1) * LANES)
        x = qk_ref[:, sl].astype(F32)
        y = x * cs + pltpu.roll(x, LANES // 2, 1) * sn
        if hd < n_q:
            y = y * q_scale
        qk_out[:, sl] = y.astype(BF16)
    csi = csi_ref[...]
    sni = sni_ref[...]
    half = idx_dim // 2
    lane = lax.broadcasted_iota(jnp.int32, csi.shape, 1)
    first = (lane % idx_dim) < half

    def rope_idx(x):
        rot = jnp.where(first, pltpu.roll(x, LANES - half, 1), pltpu.roll(x, half, 1))
        return x * csi + rot * sni

    for p in range(n_ipair):
        sl = slice(p * LANES, (p + 1) * LANES)
        qi_out[:, sl] = rope_idx(qi_ref[:, sl].astype(F32)).astype(BF16)
    kt = rope_idx(tail_ref[:, 0:LANES])
    klo = jnp.where(lane < idx_dim, kt, 0.0)
    ki_out[:, 0:LANES] = klo.astype(BF16)
    ki_out[:, LANES:2 * LANES] = pltpu.roll(klo, idx_dim, 1).astype(BF16)
    vt_out[...] = v_ref[...].astype(F32).T.astype(BF16)
    tt_out[...] = tail_ref[...].T


def _rope(cfg, main, tail, tabs, bsz, n_rows):
    m = main.shape[0]
    tm = 256
    nb = n_rows // tm
    n_q, n_kv = cfg.attn_heads, cfg.kv_heads
    qkw = cfg.qw + cfg.kvw
    assert cfg.head_dim == LANES and 2 * cfg.idx_dim == LANES
    qi_off = cfg.qw + 2 * cfg.kvw
    assert qi_off % cfg.iw == 0 and qkw % cfg.kvw == 0
    cs, sn, csi, sni = tabs
    tab_spec = pl.BlockSpec((tm, LANES), lambda i: (i % nb, 0))
    return pl.pallas_call(
        functools.partial(_rope_kernel, n_q, n_q + n_kv, cfg.iw // LANES, cfg.idx_dim, cfg.head_dim ** -0.5 * LOG2E),
        grid=(m // tm,),
        in_specs=[
            pl.BlockSpec((tm, qkw), lambda i: (i, 0)),
            pl.BlockSpec((tm, cfg.kvw), lambda i: (i, qkw // cfg.kvw)),
            pl.BlockSpec((tm, cfg.iw), lambda i: (i, qi_off // cfg.iw)),
            pl.BlockSpec((tm, TAIL_W), lambda i: (i, 0)),
            tab_spec, tab_spec, tab_spec, tab_spec,
        ],
        out_specs=[
            pl.BlockSpec((tm, qkw), lambda i: (i, 0)),
            pl.BlockSpec((tm, cfg.iw), lambda i: (i, 0)),
            pl.BlockSpec((tm, 2 * LANES), lambda i: (i, 0)),
            pl.BlockSpec((None, cfg.kvw, tm), lambda i: (i // nb, 0, i % nb)),
            pl.BlockSpec((None, TAIL_W, tm), lambda i: (i // nb, 0, i % nb)),
        ],
        out_shape=[
            jax.ShapeDtypeStruct((m, qkw), BF16),
            jax.ShapeDtypeStruct((m, cfg.iw), BF16),
            jax.ShapeDtypeStruct((m, 2 * LANES), BF16),
            jax.ShapeDtypeStruct((bsz, cfg.kvw, n_rows), BF16),
            jax.ShapeDtypeStruct((bsz, TAIL_W, n_rows), F32),
        ],
        compiler_params=_params("parallel"),
        name="rope",
    )(main, main, main, tail, cs, sn, csi, sni)


def _rope_tables(cfg, n_rows):
    pos = (jnp.arange(n_rows) - cfg.lead).astype(F32)

    def tab(dim):
        half = dim // 2
        inv = cfg.rope_theta ** (-jnp.arange(half, dtype=F32) / half)
        ang = pos[:, None] * inv[None, :]
        c, s = jnp.cos(ang), jnp.sin(ang)
        reps = LANES // dim
        return jnp.tile(jnp.concatenate([c, c], -1), (1, reps)), jnp.tile(jnp.concatenate([-s, s], -1), (1, reps))

    cs, sn = tab(cfg.head_dim)
    csi, sni = tab(cfg.idx_dim)
    return cs, sn, csi, sni


NEG_INF_KEY = -2139095041
CNT_CHAINS = 4


def _dsa_kernel(cfg, topk, qi_ref, wt_ref, ki_ref, q_ref, k_ref, vt_ref, o_ref, key_ref, qs_ref, acc_ref, m_ref, l_ref,
                al_ref, s_ref, p_ref):
    ch = cfg.chunk
    c = pl.program_id(1) + 1
    n_kc = c + 1
    rep = cfg.attn_heads // cfg.kv_heads
    hd = cfg.head_dim
    row0 = c * ch
    int_min = jnp.int32(-2 ** 31)
    sub8 = ch // 8

    q_row = row0 + lax.broadcasted_iota(jnp.int32, (ch, ch), 1)
    k_sub = lax.broadcasted_iota(jnp.int32, (ch, ch), 0)

    def admissible(koff):
        k_row = koff + k_sub
        return (k_row >= cfg.lead) & (k_row <= q_row)

    w_all = wt_ref[...] * (cfg.idx_heads ** -0.5 * cfg.idx_dim ** -0.5)

    def score_chunk(kc, carry):
        koff = pl.multiple_of(kc * ch, ch)
        k_lo = ki_ref[pl.ds(koff, ch), 0:LANES]
        k_hi = ki_ref[pl.ds(koff, ch), LANES:2 * LANES]
        acc = jnp.zeros((ch, ch), F32)
        for p in range(cfg.idx_heads // 2):
            qp = qi_ref[:, p * LANES:(p + 1) * LANES]
            for half, kk in ((0, k_lo), (1, k_hi)):
                h = 2 * p + half
                lg = lax.dot_general(kk, qp, NT, preferred_element_type=F32)
                acc = acc + jnp.maximum(lg, 0.0) * w_all[h:h + 1, :]
        sc = jnp.where(admissible(koff), acc, -jnp.inf)
        bits = lax.bitcast_convert_type(sc, jnp.int32)
        key_ref[pl.ds(koff, ch), :] = bits ^ ((bits >> 31) & jnp.int32(0x7FFFFFFF))
        return carry

    lax.fori_loop(0, n_kc, score_chunk, 0)

    key_ref[pl.ds(pl.multiple_of(n_kc * ch, ch), ch), :] = jnp.full((ch, ch), NEG_INF_KEY, jnp.int32)

    def count(pred_of_keys):
        def cnt_chunk(kc2, cnt):
            koff = pl.multiple_of(kc2 * (2 * ch), 2 * ch)
            ind = jnp.where(pred_of_keys(key_ref[pl.ds(koff, 2 * ch), :]), 1.0, 0.0)
            return cnt + jnp.sum(ind.reshape(2 * sub8 // CNT_CHAINS, CNT_CHAINS, 8, ch), axis=0)

        cnt = lax.fori_loop(0, (n_kc + 1) // 2, cnt_chunk, jnp.zeros((CNT_CHAINS, 8, ch), F32))
        return jnp.sum(jnp.sum(cnt, axis=0), axis=0, keepdims=True)

    kf = float(topk)

    def bit_step(it, state):
        tau_u, n_ge = state
        cand_u = tau_u | lax.shift_left(jnp.int32(1), jnp.int32(31) - it)
        cand = cand_u ^ int_min
        tot = count(lambda kv: kv >= cand)
        take = tot >= kf
        return jnp.where(take, cand_u, tau_u), jnp.where(take, tot, n_ge)

    tau_u, n_ge = lax.fori_loop(0, 32, bit_step, (jnp.zeros((1, ch), jnp.int32), jnp.full((1, ch), 2.0 * kf, F32)))
    tau = tau_u ^ int_min

    tied = jnp.max(jnp.where((n_ge > kf) & (tau != NEG_INF_KEY), 1.0, 0.0)) > 0.0

    @pl.when(jnp.logical_not(tied))
    def _():
        def bias_chunk(kc, carry):
            koff = pl.multiple_of(kc * ch, ch)
            sel = (key_ref[pl.ds(koff, ch), :] >= tau) & admissible(koff)
            key_ref[pl.ds(koff, ch), :] = lax.bitcast_convert_type(jnp.where(sel, 0.0, NEG_BIG).astype(F32), jnp.int32)
            return carry

        lax.fori_loop(0, n_kc, bias_chunk, 0)

    @pl.when(tied)
    def _():
        need = kf - count(lambda kv: kv > tau)
        below = (k_sub >= q_row - row0).astype(BF16)

        def bias_chunk(kc, seen):
            koff = pl.multiple_of(kc * ch, ch)
            kv = key_ref[pl.ds(koff, ch), :]
            eq = kv == tau
            rank = seen + jnp.dot(below, jnp.where(eq, 1.0, 0.0).astype(BF16), preferred_element_type=F32)
            sel = ((kv > tau) | (eq & (rank <= need))) & admissible(koff)
            key_ref[pl.ds(koff, ch), :] = lax.bitcast_convert_type(jnp.where(sel, 0.0, NEG_BIG).astype(F32), jnp.int32)
            return rank[ch - 1:ch, :]

        lax.fori_loop(0, n_kc, bias_chunk, jnp.zeros((1, ch), F32))

    for g in range(cfg.kv_heads):
        for r in range(rep):
            h = g * rep + r
            qs_ref[g, r * ch:(r + 1) * ch, :] = q_ref[:, h * hd:(h + 1) * hd]
    acc_ref[...] = jnp.zeros(acc_ref.shape, F32)
    m_ref[...] = jnp.full(m_ref.shape, NEG_BIG, F32)
    l_ref[...] = jnp.zeros(l_ref.shape, F32)

    def attn_chunk(kc, carry):
        koff = pl.multiple_of(kc * ch, ch)

        def scores(g):
            kg = k_ref[pl.ds(koff, ch), g * hd:(g + 1) * hd]
            s_ref[g % 2] = lax.dot_general(kg, qs_ref[g], NT, preferred_element_type=F32)

        def softmax_pv(g):
            for t in range(rep * ch // LANES):
                lanes = slice(t * LANES, (t + 1) * LANES)
                qlanes = slice((t * LANES) % ch, (t * LANES) % ch + LANES)
                s = s_ref[g % 2, :, lanes] + lax.bitcast_convert_type(key_ref[pl.ds(koff, ch), qlanes], F32)
                m_old = m_ref[g, 0:1, lanes]
                m_new = jnp.maximum(m_old, jnp.max(s, axis=0, keepdims=True))
                alpha = jnp.exp2(m_old - m_new)
                p = jnp.exp2(s - m_new)
                l_ref[g, 0:1, lanes] = alpha * l_ref[g, 0:1, lanes] + jnp.sum(p, axis=0, keepdims=True)
                m_ref[g, 0:1, lanes] = m_new
                al_ref[g % 2, 0:1, lanes] = alpha
                p_ref[g % 2, :, lanes] = p.astype(BF16)
            vtg = vt_ref[g * hd:(g + 1) * hd, pl.ds(koff, ch)]
            acc_ref[g] = al_ref[g % 2, 0:1, :] * acc_ref[g] + jnp.dot(vtg, p_ref[g % 2], preferred_element_type=F32)

        scores(0)
        for g in range(cfg.kv_heads):
            if g + 1 < cfg.kv_heads:
                scores(g + 1)
            softmax_pv(g)
        return carry

    lax.fori_loop(0, n_kc, attn_chunk, 0)

    for g in range(cfg.kv_heads):
        og = acc_ref[g] / l_ref[g, 0:1, :]
        for r in range(rep):
            h = g * rep + r
            o_ref[:, h * hd:(h + 1) * hd] = og[:, r * ch:(r + 1) * ch].T.astype(o_ref.dtype)


def _dsa(cfg, topk, qk_r, qi_r, ki2, vt, tail_t, bsz, n_rows):
    ch = cfg.chunk
    nq = n_rows // ch - 1
    rep = cfg.attn_heads // cfg.kv_heads
    qk3 = qk_r.reshape(bsz, n_rows, -1)
    qi3 = qi_r.reshape(bsz, n_rows, -1)
    ki3 = ki2.reshape(bsz, n_rows, -1)
    assert cfg.qw % cfg.kvw == 0 and cfg.idx_dim % cfg.idx_heads == 0 and cfg.idx_heads % 8 == 0
    k_blk = cfg.qw // cfg.kvw
    return pl.pallas_call(
        functools.partial(_dsa_kernel, cfg, topk),
        grid=(bsz, nq),
        in_specs=[
            pl.BlockSpec((None, ch, cfg.iw), lambda b, i: (b, i + 1, 0)),
            pl.BlockSpec((None, cfg.idx_heads, ch), lambda b, i: (b, cfg.idx_dim // cfg.idx_heads, i + 1)),
            pl.BlockSpec((None, n_rows, 2 * LANES), lambda b, i: (b, 0, 0)),
            pl.BlockSpec((None, ch, cfg.qw), lambda b, i: (b, i + 1, 0)),
            pl.BlockSpec((None, n_rows, cfg.kvw), lambda b, i: (b, 0, k_blk)),
            pl.BlockSpec((None, cfg.kvw, n_rows), lambda b, i: (b, 0, 0)),
        ],
        out_specs=pl.BlockSpec((None, ch, cfg.qw), lambda b, i: (b, i, 0)),
        out_shape=jax.ShapeDtypeStruct((bsz, nq * ch, cfg.qw), BF16),
        scratch_shapes=[
            pltpu.VMEM((n_rows + ch, ch), jnp.int32),
            pltpu.VMEM((cfg.kv_heads, rep * ch, cfg.head_dim), BF16),
            pltpu.VMEM((cfg.kv_heads, cfg.head_dim, rep * ch), F32),
            pltpu.VMEM((cfg.kv_heads, 8, rep * ch), F32),
            pltpu.VMEM((cfg.kv_heads, 8, rep * ch), F32),
            pltpu.VMEM((2, 8, rep * ch), F32),
            pltpu.VMEM((2, ch, rep * ch), F32),
            pltpu.VMEM((2, ch, rep * ch), BF16),
        ],
        compiler_params=_params("parallel", "arbitrary"),
        name="dsa",
    )(qi3, tail_t, ki3, qk3, qk3, vt)


def _split3(a):
    hi = a.astype(BF16)
    r1 = a - hi.astype(F32)
    mid = r1.astype(BF16)
    lo = (r1 - mid.astype(F32)).astype(BF16)
    return hi, mid, lo


def _dot3_nt(ones_lhs, a):
    return sum(lax.dot_general(ones_lhs, t, NT, preferred_element_type=F32) for t in _split3(a))


def _dot3_lhs(a, ones_rhs):
    return sum(jnp.dot(t, ones_rhs, preferred_element_type=F32) for t in _split3(a))


def _silu(x):
    return x * jax.nn.sigmoid(x)


def _ssd_kernel(cfg, z_ref, x_ref, b_ref, c_ref, dtr_ref, cwx_ref, cwb_ref, cwc_ref, cbx_ref, cbb_ref, cbc_ref,
                dbr_ref, alr_ref, dsk_ref, nw_ref, y_ref, state_ref, carry_ref, head_ref, shift_ref, tri_ref):
    ch, gw, ns, hdim, hpg = cfg.chunk, cfg.gw, cfg.d_state, cfg.ssm_head_dim, cfg.hpg
    cwid = cfg.conv_width
    c = pl.program_id(1)
    g = pl.program_id(2)
    ew = gw + 2 * ns

    @pl.when(c == 0)
    def _():
        state_ref[g] = jnp.zeros((ns, gw), F32)
        carry_ref[g] = jnp.zeros((8, ew), F32)

    @pl.when((c == 0) & (g == 0))
    def _():
        ti = lax.broadcasted_iota(jnp.int32, (ch, ch), 0)
        sj = lax.broadcasted_iota(jnp.int32, (ch, ch), 1)
        for k in range(1, cwid):
            shift_ref[k - 1] = jnp.where(sj == ti - k, 1.0, 0.0).astype(BF16)
        tri_ref[0] = jnp.where(ti >= sj, 1.0, 0.0).astype(BF16)
        tri_ref[1] = jnp.where(ti <= sj, 1.0, 0.0).astype(BF16)

    cur = jnp.concatenate([x_ref[...], b_ref[...], c_ref[...]], axis=1)
    curf = cur.astype(F32)
    head_ref[0:8, :] = carry_ref[g]
    head_ref[8:16, :] = curf[0:8, :]
    carry_ref[g] = curf[ch - 8:ch, :]
    cw_all = jnp.concatenate([cwx_ref[...], cwb_ref[...], cwc_ref[...]], axis=1)
    cb_all = jnp.concatenate([cbx_ref[...], cbb_ref[...], cbc_ref[...]], axis=1)
    acc = curf * cw_all[cwid - 1:cwid, :] + cb_all
    for k in range(1, cwid):
        sh = jnp.dot(shift_ref[k - 1], cur, preferred_element_type=F32)
        sh = jnp.concatenate([head_ref[8 - k:16 - k, :], sh[8:, :]], axis=0)
        acc = acc + sh * cw_all[cwid - 1 - k:cwid - k, :]
    conv = _silu(acc)
    xc = conv[:, 0:gw]
    bc = conv[:, gw:gw + ns]
    cc = conv[:, gw + ns:ew]

    coli = lax.broadcasted_iota(jnp.int32, (hpg, ch), 1)
    is_pad_r = (coli < cfg.lead) & (c == 0)
    dt_r = jnp.where(is_pad_r, 0.0, jax.nn.softplus(dtr_ref[...] + dbr_ref[...]))
    a_r = dt_r * (-jnp.exp(alr_ref[...]) * LOG2E)
    li = lax.broadcasted_iota(jnp.int32, (ch, ch), 0)
    si = lax.broadcasted_iota(jnp.int32, (ch, ch), 1)
    causal = li >= si
    acs_c = _dot3_nt(tri_ref[0], a_r)
    acs_r = _dot3_lhs(a_r, tri_ref[1])
    a_end = acs_r[:, ch - 1:ch]
    wdec_r = dt_r * jnp.exp2(a_end - acs_r)

    cb = lax.dot_general(cc.astype(BF16), bc.astype(BF16), NT, preferred_element_type=F32)
    cb = jnp.where(causal, cb, 0.0)
    bt = bc.T
    xb = xc.astype(BF16)
    prev = state_ref[g]
    prev_b = prev.astype(BF16)
    lane_p = lax.broadcasted_iota(jnp.int32, (ch, 2 * hdim), 1)
    lane_s = lax.broadcasted_iota(jnp.int32, (ns, 2 * hdim), 1)

    y_pairs = []
    st_pairs = []
    for pr in range(hpg // 2):
        psl = slice(pr * 2 * hdim, (pr + 1) * 2 * hdim)
        x_pair = xb[:, psl]
        prev_pair = prev_b[:, psl]
        ys, sts = [], []
        for half in range(2):
            j = 2 * pr + half
            col = acs_c[:, j:j + 1]
            row = acs_r[j:j + 1, :]
            lmat = jnp.exp2(jnp.minimum(col - row, 0.0))
            mmat = (cb * lmat * dt_r[j:j + 1, :]).astype(BF16)
            ce = (cc * jnp.exp2(col)).astype(BF16)
            ys.append(jnp.dot(mmat, x_pair, preferred_element_type=F32)
                      + jnp.dot(ce, prev_pair, preferred_element_type=F32))
            btw = (bt * wdec_r[j:j + 1, :]).astype(BF16)
            sts.append(jnp.dot(btw, x_pair, preferred_element_type=F32))
        y_pairs.append(jnp.where(lane_p < hdim, ys[0], ys[1]))
        st_pairs.append(jnp.where(lane_s < hdim, sts[0], sts[1]))
    y = jnp.concatenate(y_pairs, axis=1)
    st = jnp.concatenate(st_pairs, axis=1)

    lane_g = lax.broadcasted_iota(jnp.int32, (1, gw), 1) // hdim
    cdec = jnp.zeros((1, gw), F32)
    for j in range(hpg):
        cdec = jnp.where(lane_g == j, jnp.exp2(a_end[j:j + 1, :]), cdec)
    state_ref[g] = prev * cdec + st

    y = y + xc * dsk_ref[...]
    y = y * _silu(z_ref[...].astype(F32))
    ms = jnp.mean(y * y, axis=-1, keepdims=True)
    y_ref[...] = (y * lax.rsqrt(ms + cfg.eps) * nw_ref[...]).astype(y_ref.dtype)


def _ssd(cfg, main, tail_t, conv_w, conv_b, dt_bias, a_log, d_skip, norm_w, bsz, n_rows):
    ch, gw, ns, hpg, ng = cfg.chunk, cfg.gw, cfg.d_state, cfg.hpg, cfg.ssm_groups
    nc = n_rows // ch
    di = cfg.d_inner
    main3 = main.reshape(bsz, n_rows, -1)
    z_off = cfg.qw + 2 * cfg.kvw + cfg.iw
    x_off = z_off + di
    b_off = x_off + di
    c_off = b_off + cfg.bcw
    assert z_off % gw == 0 and b_off % ns == 0 and hpg % 2 == 0 and 2 * cfg.ssm_head_dim == LANES
    cwx, cwb, cwc = conv_w[:, :di], conv_w[:, di:di + cfg.bcw], conv_w[:, di + cfg.bcw:]
    cb2 = conv_b.reshape(1, -1)
    cbx, cbb, cbc = cb2[:, :di], cb2[:, di:di + cfg.bcw], cb2[:, di + cfg.bcw:]
    ew = gw + 2 * ns
    cw = cfg.conv_width
    grp = lambda b, c, g: (b, c, g)
    return pl.pallas_call(
        functools.partial(_ssd_kernel, cfg),
        grid=(bsz, nc, ng),
        in_specs=[
            pl.BlockSpec((None, ch, gw), lambda b, c, g: (b, c, z_off // gw + g)),
            pl.BlockSpec((None, ch, gw), lambda b, c, g: (b, c, x_off // gw + g)),
            pl.BlockSpec((None, ch, ns), lambda b, c, g: (b, c, b_off // ns + g)),
            pl.BlockSpec((None, ch, ns), lambda b, c, g: (b, c, c_off // ns + g)),
            pl.BlockSpec((None, hpg, ch), lambda b, c, g: (b, TAIL_DT // hpg + g, c)),
            pl.BlockSpec((cw, gw), lambda b, c, g: (0, g)),
            pl.BlockSpec((cw, ns), lambda b, c, g: (0, g)),
            pl.BlockSpec((cw, ns), lambda b, c, g: (0, g)),
            pl.BlockSpec((1, gw), lambda b, c, g: (0, g)),
            pl.BlockSpec((1, ns), lambda b, c, g: (0, g)),
            pl.BlockSpec((1, ns), lambda b, c, g: (0, g)),
            pl.BlockSpec((None, hpg, 1), lambda b, c, g: (g, 0, 0)),
            pl.BlockSpec((None, hpg, 1), lambda b, c, g: (g, 0, 0)),
            pl.BlockSpec((1, gw), lambda b, c, g: (0, g)),
            pl.BlockSpec((1, gw), lambda b, c, g: (0, g)),
        ],
        out_specs=pl.BlockSpec((None, ch, gw), grp),
        out_shape=jax.ShapeDtypeStruct((bsz, n_rows, di), BF16),
        scratch_shapes=[
            pltpu.VMEM((ng, ns, gw), F32),
            pltpu.VMEM((ng, 8, ew), F32),
            pltpu.VMEM((16, ew), F32),
            pltpu.VMEM((cw - 1, ch, ch), BF16),
            pltpu.VMEM((2, ch, ch), BF16),
        ],
        compiler_params=_params("arbitrary", "arbitrary", "arbitrary"),
        name="ssd",
    )(main3, main3, main3, main3, tail_t, cwx, cwb, cwc, cbx, cbb, cbc,
      dt_bias.reshape(ng, hpg, 1), a_log.reshape(ng, hpg, 1),
      jnp.repeat(d_skip, cfg.ssm_head_dim).reshape(1, di), norm_w.reshape(1, di))


def _merge_kernel(attn_ref, ssm_ref, ga_ref, gs_ref, wa_ref, ws_ref, o_ref):
    pa = jnp.dot(attn_ref[...], wa_ref[...], preferred_element_type=F32)
    ps = jnp.dot(ssm_ref[...], ws_ref[...], preferred_element_type=F32)
    ga = jax.nn.sigmoid(ga_ref[...].astype(F32))
    gs = jax.nn.sigmoid(gs_ref[...].astype(F32))
    o_ref[...] = (ga * pa + gs * ps).astype(o_ref.dtype)


def _merge(cfg, attn, ssm, main, w_a, w_s, bsz, n_rows):
    ch, d = cfg.chunk, cfg.d_model
    nq = n_rows // ch - 1
    tn = _pick(d, (1024, 512, 256))
    main3 = main.reshape(bsz, n_rows, -1)
    g_off = cfg.qw + 2 * cfg.kvw + cfg.iw + 2 * cfg.d_inner + 2 * cfg.bcw
    assert g_off % tn == 0
    return pl.pallas_call(
        _merge_kernel,
        grid=(d // tn, bsz, nq),
        in_specs=[
            pl.BlockSpec((None, ch, cfg.qw), lambda j, b, i: (b, i, 0)),
            pl.BlockSpec((None, ch, cfg.d_inner), lambda j, b, i: (b, i + 1, 0)),
            pl.BlockSpec((None, ch, tn), lambda j, b, i: (b, i + 1, g_off // tn + j)),
            pl.BlockSpec((None, ch, tn), lambda j, b, i: (b, i + 1, (g_off + d) // tn + j)),
            pl.BlockSpec((cfg.qw, tn), lambda j, b, i: (0, j)),
            pl.BlockSpec((cfg.d_inner, tn), lambda j, b, i: (0, j)),
        ],
        out_specs=pl.BlockSpec((None, ch, tn), lambda j, b, i: (b, i, j)),
        out_shape=jax.ShapeDtypeStruct((bsz, nq * ch, d), BF16),
        compiler_params=_params("parallel", "parallel", "parallel"),
        name="merge",
    )(attn, ssm, main3, main3, w_a, w_s)


def _outproj_kernel(eps, m_ref, h_ref, w_ref, nw_ref, h2_ref, hn_ref, hnt_ref):
    h2 = h_ref[...] + jnp.dot(m_ref[...], w_ref[...], preferred_element_type=F32)
    h2_ref[...] = h2
    ms = jnp.mean(h2 * h2, axis=-1, keepdims=True)
    hn = h2 * lax.rsqrt(ms + eps) * nw_ref[...]
    hn_ref[...] = hn.astype(hn_ref.dtype)
    hnt_ref[...] = hn.T.astype(hnt_ref.dtype)


def _outproj(cfg, merged, x, w_out, norm_w, bsz, n_rows):
    ch, d = cfg.chunk, cfg.d_model
    nq = n_rows // ch - 1
    return pl.pallas_call(
        functools.partial(_outproj_kernel, cfg.eps),
        grid=(bsz, nq),
        in_specs=[
            pl.BlockSpec((None, ch, d), lambda b, i: (b, i, 0)),
            pl.BlockSpec((None, ch, d), lambda b, i: (b, i, 0)),
            pl.BlockSpec((d, d), lambda b, i: (0, 0)),
            pl.BlockSpec((1, d), lambda b, i: (0, 0)),
        ],
        out_specs=[
            pl.BlockSpec((None, ch, d), lambda b, i: (b, i, 0)),
            pl.BlockSpec((None, ch, d), lambda b, i: (b, i, 0)),
            pl.BlockSpec((d, ch), lambda b, i: (0, b * nq + i)),
        ],
        out_shape=[
            jax.ShapeDtypeStruct((bsz, nq * ch, d), F32),
            jax.ShapeDtypeStruct((bsz, nq * ch, d), BF16),
            jax.ShapeDtypeStruct((d, bsz * nq * ch), BF16),
        ],
        compiler_params=_params("parallel", "parallel"),
        name="outproj",
    )(merged, x, w_out, norm_w.reshape(1, d))


def _topk_desc(s, k):
    out = []
    for _ in range(k):
        m = jnp.max(s, axis=0, keepdims=True)
        out.append(m)
        s = jnp.where(s >= m, -jnp.inf, s)
    return out


def _oddeven_merge_sort(n):
    pairs = []
    p = 1
    while p < n:
        k = p
        while k >= 1:
            for j in range(k % p, n - k, 2 * k):
                for i in range(min(k, n - j - k)):
                    if (i + j) // (2 * p) == (i + j + k) // (2 * p):
                        pairs.append((i + j, i + j + k))
            k //= 2
        p *= 2
    return pairs


def _top16_sorted(s):
    n = 16
    v = [s[r * 8:(r + 1) * 8, :] for r in range(n)]
    for i, j in _oddeven_merge_sort(n):
        v[i], v[j] = jnp.maximum(v[i], v[j]), jnp.minimum(v[i], v[j])
    for shift in (4, 2, 1):
        v = [jnp.maximum(v[r], pltpu.roll(v[n - 1 - r], shift, 0)) for r in range(n)]
        span = n // 2
        while span >= 1:
            for lo in range(n):
                if (lo // span) % 2 == 0:
                    hi = lo + span
                    v[lo], v[hi] = jnp.maximum(v[lo], v[hi]), jnp.minimum(v[lo], v[hi])
            span //= 2
    return v


def _route_kernel(cfg, x_ref, wq_ref, keys_ref, thr_ref, e0_ref, s1_ref, e1_ref):
    nk, k = cfg.n_keys, cfg.peer_topk
    half = cfg.peer_key_dim // 2
    q = jnp.dot(x_ref[...], wq_ref[...], preferred_element_type=F32)
    for h in range(cfg.peer_heads):
        st = []
        for cidx in range(2):
            off = (2 * h + cidx) * half
            qc = q[:, off:off + half].astype(BF16)
            st.append(lax.dot_general(keys_ref[cidx], qc, NT, preferred_element_type=F32))
        t0 = [t[0:1, :] for t in _top16_sorted(st[0])]
        t1 = [t[0:1, :] for t in _top16_sorted(st[1])]
        t1m = jnp.concatenate(t1, axis=0)
        cands = [a + t1m for a in t0]
        short = [cands[r][0:k // (r + 1), :] for r in range(k)]
        n_short = sum(k // (r + 1) for r in range(k))
        short.append(jnp.full((-n_short % 8, t1m.shape[1]), -jnp.inf, F32))
        best = _topk_desc(jnp.concatenate(short, axis=0), k)
        z = sum(jnp.exp(b - best[0]) for b in best)
        thr_i = jnp.full(st[0].shape, jnp.inf, F32)
        for r in range(k):
            lim = jnp.min(jnp.where(cands[r] >= best[k - 1], t1m, jnp.inf), axis=0, keepdims=True)
            thr_i = jnp.where(st[0] == t0[r], lim, thr_i)
        thr_ref[h] = thr_i
        e0_ref[h] = jnp.exp(st[0] - t0[0])
        s1_ref[h] = st[1]
        e1_ref[h] = jnp.exp(st[1] - t1[0]) / z


def _route(cfg, hn2, wq, keys):
    n, d = hn2.shape
    tm = 256
    nh, nk = cfg.peer_heads, cfg.n_keys
    qd = nh * cfg.peer_key_dim
    assert nk == LANES and cfg.peer_key_dim // 2 == LANES and cfg.peer_topk == 16
    big = pl.BlockSpec((nh, nk, tm), lambda i: (0, 0, i))
    shp = jax.ShapeDtypeStruct((nh, nk, n), F32)
    return pl.pallas_call(
        functools.partial(_route_kernel, cfg),
        grid=(n // tm,),
        in_specs=[
            pl.BlockSpec((tm, d), lambda i: (i, 0)),
            pl.BlockSpec((d, qd), lambda i: (0, 0)),
            pl.BlockSpec((2, nk, cfg.peer_key_dim // 2), lambda i: (0, 0, 0)),
        ],
        out_specs=[big, big, big, big],
        out_shape=[shp, shp, shp, shp],
        compiler_params=_params("parallel"),
        name="peer_route",
    )(hn2, wq, keys)


def _gelu(x):
    return 0.5 * x * (1.0 + lax.erf(x * (2.0 ** -0.5)))


def _gate_tile(cfg, i, lanes, thr_ref, e0_ref, s1_ref, e1_ref):
    w = None
    for h in range(cfg.peer_heads):
        sel = s1_ref[h, :, lanes] >= thr_ref[h, i:i + 1, lanes]
        term = jnp.where(sel, e1_ref[h, :, lanes], 0.0) * e0_ref[h, i:i + 1, lanes]
        w = term if w is None else w + term
    return w


MXU_ROWS = 256


def _peer_kernel(cfg, ic, n_e, xt_ref, u_ref, vt_ref, thr_ref, e0_ref, s1_ref, e1_ref, o_ref, a_ref, hb_ref):
    nk = cfg.n_keys
    d, tm = o_ref.shape
    ec = ic * nk
    s = pl.program_id(0)
    e_down = jnp.maximum(s - 2, 0) % n_e
    slot = s % 2

    @pl.when(s == 0)
    def _():
        a_ref[1] = jnp.zeros(a_ref.shape[1:], F32)
        hb_ref[...] = jnp.zeros(hb_ref.shape, BF16)

    @pl.when(e_down == 0)
    def _():
        o_ref[...] = jnp.zeros(o_ref.shape, F32)

    def up(m0):
        rows = slice(m0, m0 + MXU_ROWS)
        a_ref[slot, rows, :] = jnp.dot(u_ref[rows, :], xt_ref[...], preferred_element_type=F32)

    def down(m0):
        rows = slice(m0, m0 + MXU_ROWS)
        o_ref[rows, :] += jnp.dot(vt_ref[rows, :], hb_ref[slot], preferred_element_type=F32)

    def gate(i, c):
        rows, lanes = slice(i * nk, (i + 1) * nk), slice(c * LANES, (c + 1) * LANES)
        w = _gate_tile(cfg, i, lanes, thr_ref, e0_ref, s1_ref, e1_ref)
        hb_ref[1 - slot, rows, lanes] = (w * _gelu(a_ref[1 - slot, rows, lanes])).astype(BF16)

    mxu = [functools.partial(up, m0) for m0 in range(0, ec, MXU_ROWS)]
    mxu += [functools.partial(down, m0) for m0 in range(0, d, MXU_ROWS)]
    vec = [functools.partial(gate, i, c) for i in range(ic) for c in range(tm // LANES)]
    done = 0
    for k, piece in enumerate(mxu):
        piece()
        upto = (k + 1) * len(vec) // len(mxu)
        for tile in vec[done:upto]:
            tile()
        done = upto


def _peer(cfg, hn2t, u_b, v, routes):
    d, n = hn2t.shape
    nh, nk = cfg.peer_heads, cfg.n_keys
    tm = _pick(n, (512, 256))
    ic = 8
    ec = ic * nk
    n_e = nk // ic
    vt_b = v.reshape(n_e, ec, d).transpose(0, 2, 1).astype(BF16)
    n_steps = (n // tm) * n_e
    assert ec % MXU_ROWS == 0 and d % MXU_ROWS == 0
    back = lambda s, k: jnp.clip(s - k, 0, n_steps - 1)
    chunked = pl.BlockSpec((nh, ic, tm), lambda s: (0, back(s, 1) % n_e, back(s, 1) // n_e))
    full = pl.BlockSpec((nh, nk, tm), lambda s: (0, 0, back(s, 1) // n_e))
    return pl.pallas_call(
        functools.partial(_peer_kernel, cfg, ic, n_e),
        grid=(n_steps + 2,),
        in_specs=[
            pl.BlockSpec((d, tm), lambda s: (0, back(s, 0) // n_e)),
            pl.BlockSpec((ec, d), lambda s: (back(s, 0) % n_e, 0)),
            pl.BlockSpec((None, d, ec), lambda s: (back(s, 2) % n_e, 0, 0)),
            chunked, chunked, full, full,
        ],
        out_specs=pl.BlockSpec((d, tm), lambda s: (0, back(s, 2) // n_e)),
        out_shape=jax.ShapeDtypeStruct((d, n), F32),
        scratch_shapes=[pltpu.VMEM((2, ec, tm), F32), pltpu.VMEM((2, ec, tm), BF16)],
        compiler_params=_params("arbitrary"),
        name="peer_experts",
    )(hn2t, u_b, vt_b, *routes)


def _final_kernel(eps, h2_ref, pt_ref, nw_ref, o_ref):
    h = h2_ref[...] + pt_ref[...].T
    ms = jnp.mean(h * h, axis=-1, keepdims=True)
    o_ref[...] = h * lax.rsqrt(ms + eps) * nw_ref[...]


def _final(cfg, h2, peer_t, norm_w):
    n, d = h2.shape
    tm = 256
    return pl.pallas_call(
        functools.partial(_final_kernel, cfg.eps),
        grid=(n // tm,),
        in_specs=[
            pl.BlockSpec((tm, d), lambda i: (i, 0)),
            pl.BlockSpec((d, tm), lambda i: (0, i)),
            pl.BlockSpec((1, d), lambda i: (0, 0)),
        ],
        out_specs=pl.BlockSpec((tm, d), lambda i: (i, 0)),
        out_shape=jax.ShapeDtypeStruct((n, d), F32),
        compiler_params=_params("parallel"),
        name="final_norm",
    )(h2, peer_t, norm_w.reshape(1, d))


def _layer(cfg, x, meta_tokens, norm_mix_w, w_in, conv_w, conv_b, dt_bias, a_log, d_skip, ssm_norm_w,
           w_branch_attn, w_branch_ssm, w_out, norm_ffn_w, peer_w_query, peer_sub_keys, peer_u, peer_v,
           norm_final_w):
    bsz, seq, d = x.shape
    ch = cfg.chunk
    assert seq % ch == 0 and norm_mix_w.shape[0] == 1
    n_rows = ch + seq
    topk = min(cfg.topk_max, seq // 4)
    w_t = w_in[0].T
    o_iq = cfg.qw + 2 * cfg.kvw
    o_ik = o_iq + cfg.iw
    o_iw = o_ik + cfg.idx_dim
    o_z = o_iw + cfg.idx_heads
    o_dt = o_z + 2 * cfg.d_inner + 2 * cfg.bcw
    o_g = o_dt + cfg.ssm_heads
    segments = ((0, o_ik), (o_z, o_dt - o_z), (o_g, w_t.shape[0] - o_g))
    zpad = lambda n: jnp.zeros((n, d), w_t.dtype)
    w_tail_t = jnp.concatenate([
        w_t[o_ik:o_z], zpad(TAIL_DT - cfg.idx_dim - cfg.idx_heads),
        w_t[o_dt:o_g], zpad(TAIL_W - TAIL_DT - cfg.ssm_heads)], axis=0)

    hn = _prenorm(cfg, x, meta_tokens, norm_mix_w[0])
    main, tail = _inproj(cfg, hn.reshape(bsz * n_rows, d), w_t, w_tail_t, segments)
    qk_r, qi_r, ki2, vt, tail_t = _rope(cfg, main, tail, _rope_tables(cfg, n_rows), bsz, n_rows)
    attn = _dsa(cfg, topk, qk_r, qi_r, ki2, vt, tail_t, bsz, n_rows)
    ssm = _ssd(cfg, main, tail_t, conv_w[0], conv_b[0], dt_bias[0], a_log[0], d_skip[0], ssm_norm_w[0], bsz, n_rows)
    merged = _merge(cfg, attn, ssm, main, w_branch_attn[0].astype(BF16), w_branch_ssm[0].astype(BF16), bsz, n_rows)
    h2, hn2, hn2t = _outproj(cfg, merged, x, w_out[0].astype(BF16), norm_ffn_w[0], bsz, n_rows)
    h2 = h2.reshape(bsz * seq, d)
    hn2 = hn2.reshape(bsz * seq, d)
    routes = _route(cfg, hn2, peer_w_query[0].astype(BF16), peer_sub_keys[0].astype(BF16))
    peer_t = _peer(cfg, hn2t, peer_u[0].astype(BF16), peer_v[0], routes)
    out = _final(cfg, h2, peer_t, norm_final_w)
    return out.reshape(bsz, seq, d)


def kernel(x, meta_tokens, norm_mix_w, w_in, conv_w, conv_b, dt_bias, a_log, d_skip, ssm_norm_w, w_branch_attn,
           w_branch_ssm, w_out, norm_ffn_w, peer_w_query, peer_sub_keys, peer_u, peer_v, norm_final_w):
    return _layer(Cfg(), x, meta_tokens, norm_mix_w, w_in, conv_w, conv_b, dt_bias, a_log, d_skip, ssm_norm_w,
                  w_branch_attn, w_branch_ssm, w_out, norm_ffn_w, peer_w_query, peer_sub_keys, peer_u, peer_v,
                  norm_final_w)
```

```python
import functools
import math
from typing import NamedTuple

import jax
import jax.numpy as jnp
from jax import lax
from jax.experimental import pallas as pl
from jax.experimental.pallas import tpu as pltpu

F32 = jnp.float32
BF16 = jnp.bfloat16
LANES = 128
LOG2E = math.log2(math.e)
NEG_BIG = -1e30
VMEM_LIMIT = 56 * 1024 * 1024


class Cfg(NamedTuple):
    d_model: int = 2048
    n_meta: int = 16
    attn_heads: int = 16
    kv_heads: int = 4
    head_dim: int = 128
    idx_heads: int = 16
    idx_dim: int = 64
    topk_max: int = 256
    rope_theta: float = 10000.0
    d_inner: int = 4096
    ssm_head_dim: int = 64
    ssm_groups: int = 8
    d_state: int = 128
    conv_width: int = 4
    chunk: int = 256
    peer_heads: int = 8
    n_keys: int = 128
    peer_key_dim: int = 256
    peer_topk: int = 16
    eps: float = 1e-6

    @property
    def lead(self):
        return self.chunk - self.n_meta

    @property
    def qw(self):
        return self.attn_heads * self.head_dim

    @property
    def kvw(self):
        return self.kv_heads * self.head_dim

    @property
    def iw(self):
        return self.idx_heads * self.idx_dim

    @property
    def ssm_heads(self):
        return self.d_inner // self.ssm_head_dim

    @property
    def hpg(self):
        return self.ssm_heads // self.ssm_groups

    @property
    def gw(self):
        return self.d_inner // self.ssm_groups

    @property
    def bcw(self):
        return self.ssm_groups * self.d_state


NT = (((1,), (1,)), ((), ()))
TAIL_W = 256
TAIL_DT = 128


def _params(*sem):
    return pltpu.CompilerParams(dimension_semantics=sem, vmem_limit_bytes=VMEM_LIMIT)


def _rms(x, w, eps):
    ms = jnp.mean(x * x, axis=-1, keepdims=True)
    return x * lax.rsqrt(ms + eps) * w


def _prenorm_kernel(cfg, x_ref, meta_ref, nw_ref, o_ref):
    c = pl.program_id(1)

    @pl.when(c == 0)
    def _():
        o_ref[0:cfg.lead, :] = jnp.zeros((cfg.lead, o_ref.shape[1]), o_ref.dtype)
        o_ref[cfg.lead:cfg.chunk, :] = _rms(meta_ref[...], nw_ref[...], cfg.eps).astype(o_ref.dtype)

    @pl.when(c > 0)
    def _():
        o_ref[...] = _rms(x_ref[...], nw_ref[...], cfg.eps).astype(o_ref.dtype)


def _prenorm(cfg, x, meta_tokens, norm_w):
    bsz, seq, d = x.shape
    ch = cfg.chunk
    nc = seq // ch + 1
    return pl.pallas_call(
        functools.partial(_prenorm_kernel, cfg),
        grid=(bsz, nc),
        in_specs=[
            pl.BlockSpec((None, ch, d), lambda b, c: (b, jnp.maximum(c - 1, 0), 0)),
            pl.BlockSpec((cfg.n_meta, d), lambda b, c: (0, 0)),
            pl.BlockSpec((1, d), lambda b, c: (0, 0)),
        ],
        out_specs=pl.BlockSpec((None, ch, d), lambda b, c: (b, c, 0)),
        out_shape=jax.ShapeDtypeStruct((bsz, nc * ch, d), BF16),
        compiler_params=_params("parallel", "parallel"),
        name="prenorm",
    )(x, meta_tokens.astype(x.dtype), norm_w.reshape(1, d))


def _inproj_kernel(hn_ref, w_ref, wt_ref, main_ref, tail_ref):
    @pl.when(pl.program_id(1) == 0)
    def _():
        tail_ref[...] = lax.dot_general(hn_ref[...], wt_ref[...].astype(BF16), NT, preferred_element_type=F32)

    w = w_ref[...].astype(BF16)
    main_ref[...] = lax.dot_general(hn_ref[...], w, NT, preferred_element_type=F32).astype(main_ref.dtype)


def _pick(n, cands):
    for c in cands:
        if n % c == 0:
            return c
    raise ValueError(f"no tile for {n}")


def _inproj(cfg, hn2d, w_t, w_tail_t, segments):
    m, d = hn2d.shape
    nm = sum(n for _, n in segments)
    tm = _pick(m, (1088, 512, 256))
    tn = _pick(math.gcd(*[n for _, n in segments]), (1024, 512, 256))

    sub = 8
    assert all(start % sub == 0 and n % tn == 0 for start, n in segments)

    def w_row(j):
        row = j * (tn // sub)
        done = 0
        for start, n in segments:
            shift = (start - done) // sub
            row = row + jnp.where(j >= done // tn, shift, 0) - jnp.where(j >= (done + n) // tn, shift, 0)
            done += n
        return row * sub

    return pl.pallas_call(
        _inproj_kernel,
        grid=(m // tm, nm // tn),
        in_specs=[
            pl.BlockSpec((tm, d), lambda i, j: (i, 0)),
            pl.BlockSpec((pl.Element(tn), pl.Element(d)), lambda i, j: (w_row(j), 0)),
            pl.BlockSpec((TAIL_W, d), lambda i, j: (0, 0)),
        ],
        out_specs=[
            pl.BlockSpec((tm, tn), lambda i, j: (i, j)),
            pl.BlockSpec((tm, TAIL_W), lambda i, j: (i, 0)),
        ],
        out_shape=[
            jax.ShapeDtypeStruct((m, nm), BF16),
            jax.ShapeDtypeStruct((m, TAIL_W), F32),
        ],
        compiler_params=_params("parallel", "arbitrary"),
        name="inproj",
    )(hn2d, w_t, w_tail_t)


def _rope_kernel(n_q, n_qk, n_ipair, idx_dim, q_scale, qk_ref, v_ref, qi_ref, tail_ref, cs_ref, sn_ref, csi_ref,
                 sni_ref, qk_out, qi_out, ki_out, vt_out, tt_out):
    cs = cs_ref[...]
    sn = sn_ref[...]
    for hd in range(n_qk):
        sl = slice(hd * LANES, (hd + 1) * LANES)
        x = qk_ref[:, sl].astype(F32)
        y = x * cs + pltpu.roll(x, LANES // 2, 1) * sn
        if hd < n_q:
            y = y * q_scale
        qk_out[:, sl] = y.astype(BF16)
    csi = csi_ref[...]
    sni = sni_ref[...]
    half = idx_dim // 2
    lane = lax.broadcasted_iota(jnp.int32, csi.shape, 1)
    first = (lane % idx_dim) < half

    def rope_idx(x):
        rot = jnp.where(first, pltpu.roll(x, LANES - half, 1), pltpu.roll(x, half, 1))
        return x * csi + rot * sni

    for p in range(n_ipair):
        sl = slice(p * LANES, (p + 1) * LANES)
        qi_out[:, sl] = rope_idx(qi_ref[:, sl].astype(F32)).astype(BF16)
    kt = rope_idx(tail_ref[:, 0:LANES])
    klo = jnp.where(lane < idx_dim, kt, 0.0)
    ki_out[:, 0:LANES] = klo.astype(BF16)
    ki_out[:, LANES:2 * LANES] = pltpu.roll(klo, idx_dim, 1).astype(BF16)
    vt_out[...] = v_ref[...].astype(F32).T.astype(BF16)
    tt_out[...] = tail_ref[...].T


def _rope(cfg, main, tail, tabs, bsz, n_rows):
    m = main.shape[0]
    tm = 256
    nb = n_rows // tm
    n_q, n_kv = cfg.attn_heads, cfg.kv_heads
    qkw = cfg.qw + cfg.kvw
    assert cfg.head_dim == LANES and 2 * cfg.idx_dim == LANES
    qi_off = cfg.qw + 2 * cfg.kvw
    assert qi_off % cfg.iw == 0 and qkw % cfg.kvw == 0
    cs, sn, csi, sni = tabs
    tab_spec = pl.BlockSpec((tm, LANES), lambda i: (i % nb, 0))
    return pl.pallas_call(
        functools.partial(_rope_kernel, n_q, n_q + n_kv, cfg.iw // LANES, cfg.idx_dim, cfg.head_dim ** -0.5 * LOG2E),
        grid=(m // tm,),
        in_specs=[
            pl.BlockSpec((tm, qkw), lambda i: (i, 0)),
            pl.BlockSpec((tm, cfg.kvw), lambda i: (i, qkw // cfg.kvw)),
            pl.BlockSpec((tm, cfg.iw), lambda i: (i, qi_off // cfg.iw)),
            pl.BlockSpec((tm, TAIL_W), lambda i: (i, 0)),
            tab_spec, tab_spec, tab_spec, tab_spec,
        ],
        out_specs=[
            pl.BlockSpec((tm, qkw), lambda i: (i, 0)),
            pl.BlockSpec((tm, cfg.iw), lambda i: (i, 0)),
            pl.BlockSpec((tm, 2 * LANES), lambda i: (i, 0)),
            pl.BlockSpec((None, cfg.kvw, tm), lambda i: (i // nb, 0, i % nb)),
            pl.BlockSpec((None, TAIL_W, tm), lambda i: (i // nb, 0, i % nb)),
        ],
        out_shape=[
            jax.ShapeDtypeStruct((m, qkw), BF16),
            jax.ShapeDtypeStruct((m, cfg.iw), BF16),
            jax.ShapeDtypeStruct((m, 2 * LANES), BF16),
            jax.ShapeDtypeStruct((bsz, cfg.kvw, n_rows), BF16),
            jax.ShapeDtypeStruct((bsz, TAIL_W, n_rows), F32),
        ],
        compiler_params=_params("parallel"),
        name="rope",
    )(main, main, main, tail, cs, sn, csi, sni)


def _rope_tables(cfg, n_rows):
    pos = (jnp.arange(n_rows) - cfg.lead).astype(F32)

    def tab(dim):
        half = dim // 2
        inv = cfg.rope_theta ** (-jnp.arange(half, dtype=F32) / half)
        ang = pos[:, None] * inv[None, :]
        c, s = jnp.cos(ang), jnp.sin(ang)
        reps = LANES // dim
        return jnp.tile(jnp.concatenate([c, c], -1), (1, reps)), jnp.tile(jnp.concatenate([-s, s], -1), (1, reps))

    cs, sn = tab(cfg.head_dim)
    csi, sni = tab(cfg.idx_dim)
    return cs, sn, csi, sni


NEG_INF_KEY = -2139095041
CNT_CHAINS = 4


def _dsa_kernel(cfg, topk, qi_ref, wt_ref, ki_ref, q_ref, k_ref, vt_ref, o_ref, key_ref, qs_ref, acc_ref, m_ref, l_ref,
                al_ref, s_ref, p_ref, qst_ref):
    ch = cfg.chunk
    c = pl.program_id(1) + 1
    n_kc = c + 1
    rep = cfg.attn_heads // cfg.kv_heads
    hd = cfg.head_dim
    row0 = c * ch
    int_min = jnp.int32(-2 ** 31)
    sub8 = ch // 8

    q_row = row0 + lax.broadcasted_iota(jnp.int32, (ch, ch), 1)
    k_sub = lax.broadcasted_iota(jnp.int32, (ch, ch), 0)

    def admissible(koff):
        k_row = koff + k_sub
        return (k_row >= cfg.lead) & (k_row <= q_row)

    w_all = wt_ref[...] * (cfg.idx_heads ** -0.5 * cfg.idx_dim ** -0.5)

    for p in range(cfg.idx_heads // 2):
        qst_ref[p * ch:(p + 1) * ch, :] = qi_ref[:, p * LANES:(p + 1) * LANES]

    def score_chunk(kc, carry):
        koff = pl.multiple_of(kc * ch, ch)
        k_lo = ki_ref[pl.ds(koff, ch), 0:LANES]
        k_hi = ki_ref[pl.ds(koff, ch), LANES:2 * LANES]
        n_pair = cfg.idx_heads // 2
        acc = jnp.zeros((ch, ch), F32)
        for half, kk in ((0, k_lo), (1, k_hi)):
            lg = lax.dot_general(kk, qst_ref[...], NT, preferred_element_type=F32)
            for p in range(n_pair):
                h = 2 * p + half
                acc = acc + jnp.maximum(lg[:, p * ch:(p + 1) * ch], 0.0) * w_all[h:h + 1, :]
        sc = jnp.where(admissible(koff), acc, -jnp.inf)
        bits = lax.bitcast_convert_type(sc, jnp.int32)
        key_ref[pl.ds(koff, ch), :] = bits ^ ((bits >> 31) & jnp.int32(0x7FFFFFFF))
        return carry

    lax.fori_loop(0, n_kc, score_chunk, 0)

    key_ref[pl.ds(pl.multiple_of(n_kc * ch, ch), ch), :] = jnp.full((ch, ch), NEG_INF_KEY, jnp.int32)

    def count(pred_of_keys):
        def cnt_chunk(kc2, cnt):
            koff = pl.multiple_of(kc2 * (2 * ch), 2 * ch)
            ind = jnp.where(pred_of_keys(key_ref[pl.ds(koff, 2 * ch), :]), 1.0, 0.0)
            return cnt + jnp.sum(ind.reshape(2 * sub8 // CNT_CHAINS, CNT_CHAINS, 8, ch), axis=0)

        cnt = lax.fori_loop(0, (n_kc + 1) // 2, cnt_chunk, jnp.zeros((CNT_CHAINS, 8, ch), F32))
        return jnp.sum(jnp.sum(cnt, axis=0), axis=0, keepdims=True)

    kf = float(topk)

    def bit_step(it, state):
        tau_u, n_ge = state
        cand_u = tau_u | lax.shift_left(jnp.int32(1), jnp.int32(31) - it)
        cand = cand_u ^ int_min
        tot = count(lambda kv: kv >= cand)
        take = tot >= kf
        return jnp.where(take, cand_u, tau_u), jnp.where(take, tot, n_ge)

    tau_u, n_ge = lax.fori_loop(0, 32, bit_step, (jnp.zeros((1, ch), jnp.int32), jnp.full((1, ch), 2.0 * kf, F32)))
    tau = tau_u ^ int_min

    tied = jnp.max(jnp.where((n_ge > kf) & (tau != NEG_INF_KEY), 1.0, 0.0)) > 0.0

    @pl.when(jnp.logical_not(tied))
    def _():
        def bias_chunk(kc, carry):
            koff = pl.multiple_of(kc * ch, ch)
            sel = (key_ref[pl.ds(koff, ch), :] >= tau) & admissible(koff)
            key_ref[pl.ds(koff, ch), :] = lax.bitcast_convert_type(jnp.where(sel, 0.0, NEG_BIG).astype(F32), jnp.int32)
            return carry

        lax.fori_loop(0, n_kc, bias_chunk, 0)

    @pl.when(tied)
    def _():
        need = kf - count(lambda kv: kv > tau)
        below = (k_sub >= q_row - row0).astype(BF16)

        def bias_chunk(kc, seen):
            koff = pl.multiple_of(kc * ch, ch)
            kv = key_ref[pl.ds(koff, ch), :]
            eq = kv == tau
            rank = seen + jnp.dot(below, jnp.where(eq, 1.0, 0.0).astype(BF16), preferred_element_type=F32)
            sel = ((kv > tau) | (eq & (rank <= need))) & admissible(koff)
            key_ref[pl.ds(koff, ch), :] = lax.bitcast_convert_type(jnp.where(sel, 0.0, NEG_BIG).astype(F32), jnp.int32)
            return rank[ch - 1:ch, :]

        lax.fori_loop(0, n_kc, bias_chunk, jnp.zeros((1, ch), F32))

    for g in range(cfg.kv_heads):
        for r in range(rep):
            h = g * rep + r
            qs_ref[g, r * ch:(r + 1) * ch, :] = q_ref[:, h * hd:(h + 1) * hd]
    acc_ref[...] = jnp.zeros(acc_ref.shape, F32)
    m_ref[...] = jnp.full(m_ref.shape, NEG_BIG, F32)
    l_ref[...] = jnp.zeros(l_ref.shape, F32)

    def attn_chunk(kc, carry):
        koff = pl.multiple_of(kc * ch, ch)

        def scores(g):
            kg = k_ref[pl.ds(koff, ch), g * hd:(g + 1) * hd]
            s_ref[g % 2] = lax.dot_general(kg, qs_ref[g], NT, preferred_element_type=F32)

        def softmax_pv(g):
            for t in range(rep * ch // LANES):
                lanes = slice(t * LANES, (t + 1) * LANES)
                qlanes = slice((t * LANES) % ch, (t * LANES) % ch + LANES)
                s = s_ref[g % 2, :, lanes] + lax.bitcast_convert_type(key_ref[pl.ds(koff, ch), qlanes], F32)
                m_old = m_ref[g, 0:1, lanes]
                m_new = jnp.maximum(m_old, jnp.max(s, axis=0, keepdims=True))
                alpha = jnp.exp2(m_old - m_new)
                p = jnp.exp2(s - m_new)
                l_ref[g, 0:1, lanes] = alpha * l_ref[g, 0:1, lanes] + jnp.sum(p, axis=0, keepdims=True)
                m_ref[g, 0:1, lanes] = m_new
                al_ref[g % 2, 0:1, lanes] = alpha
                p_ref[g % 2, :, lanes] = p.astype(BF16)
            vtg = vt_ref[g * hd:(g + 1) * hd, pl.ds(koff, ch)]
            acc_ref[g] = al_ref[g % 2, 0:1, :] * acc_ref[g] + jnp.dot(vtg, p_ref[g % 2], preferred_element_type=F32)

        scores(0)
        for g in range(cfg.kv_heads):
            if g + 1 < cfg.kv_heads:
                scores(g + 1)
            softmax_pv(g)
        return carry

    lax.fori_loop(0, n_kc, attn_chunk, 0)

    for g in range(cfg.kv_heads):
        og = acc_ref[g] / l_ref[g, 0:1, :]
        for r in range(rep):
            h = g * rep + r
            o_ref[:, h * hd:(h + 1) * hd] = og[:, r * ch:(r + 1) * ch].T.astype(o_ref.dtype)


def _dsa(cfg, topk, qk_r, qi_r, ki2, vt, tail_t, bsz, n_rows):
    ch = cfg.chunk
    nq = n_rows // ch - 1
    rep = cfg.attn_heads // cfg.kv_heads
    qk3 = qk_r.reshape(bsz, n_rows, -1)
    qi3 = qi_r.reshape(bsz, n_rows, -1)
    ki3 = ki2.reshape(bsz, n_rows, -1)
    assert cfg.qw % cfg.kvw == 0 and cfg.idx_dim % cfg.idx_heads == 0 and cfg.idx_heads % 8 == 0
    k_blk = cfg.qw // cfg.kvw
    return pl.pallas_call(
        functools.partial(_dsa_kernel, cfg, topk),
        grid=(bsz, nq),
        in_specs=[
            pl.BlockSpec((None, ch, cfg.iw), lambda b, i: (b, i + 1, 0)),
            pl.BlockSpec((None, cfg.idx_heads, ch), lambda b, i: (b, cfg.idx_dim // cfg.idx_heads, i + 1)),
            pl.BlockSpec((None, n_rows, 2 * LANES), lambda b, i: (b, 0, 0)),
            pl.BlockSpec((None, ch, cfg.qw), lambda b, i: (b, i + 1, 0)),
            pl.BlockSpec((None, n_rows, cfg.kvw), lambda b, i: (b, 0, k_blk)),
            pl.BlockSpec((None, cfg.kvw, n_rows), lambda b, i: (b, 0, 0)),
        ],
        out_specs=pl.BlockSpec((None, ch, cfg.qw), lambda b, i: (b, i, 0)),
        out_shape=jax.ShapeDtypeStruct((bsz, nq * ch, cfg.qw), BF16),
        scratch_shapes=[
            pltpu.VMEM((n_rows + ch, ch), jnp.int32),
            pltpu.VMEM((cfg.kv_heads, rep * ch, cfg.head_dim), BF16),
            pltpu.VMEM((cfg.kv_heads, cfg.head_dim, rep * ch), F32),
            pltpu.VMEM((cfg.kv_heads, 8, rep * ch), F32),
            pltpu.VMEM((cfg.kv_heads, 8, rep * ch), F32),
            pltpu.VMEM((2, 8, rep * ch), F32),
            pltpu.VMEM((2, ch, rep * ch), F32),
            pltpu.VMEM((2, ch, rep * ch), BF16),
            pltpu.VMEM((cfg.idx_heads // 2 * ch, LANES), BF16),
        ],
        compiler_params=_params("parallel", "arbitrary"),
        name="dsa",
    )(qi3, tail_t, ki3, qk3, qk3, vt)


def _split3(a):
    hi = a.astype(BF16)
    r1 = a - hi.astype(F32)
    mid = r1.astype(BF16)
    lo = (r1 - mid.astype(F32)).astype(BF16)
    return hi, mid, lo


def _dot3_nt(ones_lhs, a):
    return sum(lax.dot_general(ones_lhs, t, NT, preferred_element_type=F32) for t in _split3(a))


def _dot3_lhs(a, ones_rhs):
    return sum(jnp.dot(t, ones_rhs, preferred_element_type=F32) for t in _split3(a))


def _silu(x):
    return x * jax.nn.sigmoid(x)


def _ssd_kernel(cfg, z_ref, x_ref, b_ref, c_ref, dtr_ref, cwx_ref, cwb_ref, cwc_ref, cbx_ref, cbb_ref, cbc_ref,
                dbr_ref, alr_ref, dsk_ref, nw_ref, y_ref, state_ref, carry_ref, head_ref, shift_ref, tri_ref):
    ch, gw, ns, hdim, hpg = cfg.chunk, cfg.gw, cfg.d_state, cfg.ssm_head_dim, cfg.hpg
    cwid = cfg.conv_width
    c = pl.program_id(1)
    g = pl.program_id(2)
    ew = gw + 2 * ns

    @pl.when(c == 0)
    def _():
        state_ref[g] = jnp.zeros((ns, gw), F32)
        carry_ref[g] = jnp.zeros((8, ew), F32)

    @pl.when((c == 0) & (g == 0))
    def _():
        ti = lax.broadcasted_iota(jnp.int32, (ch, ch), 0)
        sj = lax.broadcasted_iota(jnp.int32, (ch, ch), 1)
        for k in range(1, cwid):
            shift_ref[k - 1] = jnp.where(sj == ti - k, 1.0, 0.0).astype(BF16)
        tri_ref[0] = jnp.where(ti >= sj, 1.0, 0.0).astype(BF16)
        tri_ref[1] = jnp.where(ti <= sj, 1.0, 0.0).astype(BF16)

    cur = jnp.concatenate([x_ref[...], b_ref[...], c_ref[...]], axis=1)
    curf = cur.astype(F32)
    head_ref[0:8, :] = carry_ref[g]
    head_ref[8:16, :] = curf[0:8, :]
    carry_ref[g] = curf[ch - 8:ch, :]
    cw_all = jnp.concatenate([cwx_ref[...], cwb_ref[...], cwc_ref[...]], axis=1)
    cb_all = jnp.concatenate([cbx_ref[...], cbb_ref[...], cbc_ref[...]], axis=1)
    acc = curf * cw_all[cwid - 1:cwid, :] + cb_all
    for k in range(1, cwid):
        sh = jnp.dot(shift_ref[k - 1], cur, preferred_element_type=F32)
        sh = jnp.concatenate([head_ref[8 - k:16 - k, :], sh[8:, :]], axis=0)
        acc = acc + sh * cw_all[cwid - 1 - k:cwid - k, :]
    conv = _silu(acc)
    xc = conv[:, 0:gw]
    bc = conv[:, gw:gw + ns]
    cc = conv[:, gw + ns:ew]

    coli = lax.broadcasted_iota(jnp.int32, (hpg, ch), 1)
    is_pad_r = (coli < cfg.lead) & (c == 0)
    dt_r = jnp.where(is_pad_r, 0.0, jax.nn.softplus(dtr_ref[...] + dbr_ref[...]))
    a_r = dt_r * (-jnp.exp(alr_ref[...]) * LOG2E)
    li = lax.broadcasted_iota(jnp.int32, (ch, ch), 0)
    si = lax.broadcasted_iota(jnp.int32, (ch, ch), 1)
    causal = li >= si
    acs_c = _dot3_nt(tri_ref[0], a_r)
    acs_r = _dot3_lhs(a_r, tri_ref[1])
    a_end = acs_r[:, ch - 1:ch]
    wdec_r = dt_r * jnp.exp2(a_end - acs_r)

    cb = lax.dot_general(cc.astype(BF16), bc.astype(BF16), NT, preferred_element_type=F32)
    cb = jnp.where(causal, cb, 0.0)
    bt = bc.T
    xb = xc.astype(BF16)
    prev = state_ref[g]
    prev_b = prev.astype(BF16)
    lane_p = lax.broadcasted_iota(jnp.int32, (ch, 2 * hdim), 1)
    lane_s = lax.broadcasted_iota(jnp.int32, (ns, 2 * hdim), 1)

    y_pairs = []
    st_pairs = []
    for pr in range(hpg // 2):
        psl = slice(pr * 2 * hdim, (pr + 1) * 2 * hdim)
        x_pair = xb[:, psl]
        prev_pair = prev_b[:, psl]
        ys, sts = [], []
        for half in range(2):
            j = 2 * pr + half
            col = acs_c[:, j:j + 1]
            row = acs_r[j:j + 1, :]
            lmat = jnp.exp2(jnp.minimum(col - row, 0.0))
            mmat = (cb * lmat * dt_r[j:j + 1, :]).astype(BF16)
            ce = (cc * jnp.exp2(col)).astype(BF16)
            ys.append(jnp.dot(mmat, x_pair, preferred_element_type=F32)
                      + jnp.dot(ce, prev_pair, preferred_element_type=F32))
            btw = (bt * wdec_r[j:j + 1, :]).astype(BF16)
            sts.append(jnp.dot(btw, x_pair, preferred_element_type=F32))
        y_pairs.append(jnp.where(lane_p < hdim, ys[0], ys[1]))
        st_pairs.append(jnp.where(lane_s < hdim, sts[0], sts[1]))
    y = jnp.concatenate(y_pairs, axis=1)
    st = jnp.concatenate(st_pairs, axis=1)

    lane_g = lax.broadcasted_iota(jnp.int32, (1, gw), 1) // hdim
    cdec = jnp.zeros((1, gw), F32)
    for j in range(hpg):
        cdec = jnp.where(lane_g == j, jnp.exp2(a_end[j:j + 1, :]), cdec)
    state_ref[g] = prev * cdec + st

    y = y + xc * dsk_ref[...]
    y = y * _silu(z_ref[...].astype(F32))
    ms = jnp.mean(y * y, axis=-1, keepdims=True)
    y_ref[...] = (y * lax.rsqrt(ms + cfg.eps) * nw_ref[...]).astype(y_ref.dtype)


def _ssd(cfg, main, tail_t, conv_w, conv_b, dt_bias, a_log, d_skip, norm_w, bsz, n_rows):
    ch, gw, ns, hpg, ng = cfg.chunk, cfg.gw, cfg.d_state, cfg.hpg, cfg.ssm_groups
    nc = n_rows // ch
    di = cfg.d_inner
    main3 = main.reshape(bsz, n_rows, -1)
    z_off = cfg.qw + 2 * cfg.kvw + cfg.iw
    x_off = z_off + di
    b_off = x_off + di
    c_off = b_off + cfg.bcw
    assert z_off % gw == 0 and b_off % ns == 0 and hpg % 2 == 0 and 2 * cfg.ssm_head_dim == LANES
    cwx, cwb, cwc = conv_w[:, :di], conv_w[:, di:di + cfg.bcw], conv_w[:, di + cfg.bcw:]
    cb2 = conv_b.reshape(1, -1)
    cbx, cbb, cbc = cb2[:, :di], cb2[:, di:di + cfg.bcw], cb2[:, di + cfg.bcw:]
    ew = gw + 2 * ns
    cw = cfg.conv_width
    grp = lambda b, c, g: (b, c, g)
    return pl.pallas_call(
        functools.partial(_ssd_kernel, cfg),
        grid=(bsz, nc, ng),
        in_specs=[
            pl.BlockSpec((None, ch, gw), lambda b, c, g: (b, c, z_off // gw + g)),
            pl.BlockSpec((None, ch, gw), lambda b, c, g: (b, c, x_off // gw + g)),
            pl.BlockSpec((None, ch, ns), lambda b, c, g: (b, c, b_off // ns + g)),
            pl.BlockSpec((None, ch, ns), lambda b, c, g: (b, c, c_off // ns + g)),
            pl.BlockSpec((None, hpg, ch), lambda b, c, g: (b, TAIL_DT // hpg + g, c)),
            pl.BlockSpec((cw, gw), lambda b, c, g: (0, g)),
            pl.BlockSpec((cw, ns), lambda b, c, g: (0, g)),
            pl.BlockSpec((cw, ns), lambda b, c, g: (0, g)),
            pl.BlockSpec((1, gw), lambda b, c, g: (0, g)),
            pl.BlockSpec((1, ns), lambda b, c, g: (0, g)),
            pl.BlockSpec((1, ns), lambda b, c, g: (0, g)),
            pl.BlockSpec((None, hpg, 1), lambda b, c, g: (g, 0, 0)),
            pl.BlockSpec((None, hpg, 1), lambda b, c, g: (g, 0, 0)),
            pl.BlockSpec((1, gw), lambda b, c, g: (0, g)),
            pl.BlockSpec((1, gw), lambda b, c, g: (0, g)),
        ],
        out_specs=pl.BlockSpec((None, ch, gw), grp),
        out_shape=jax.ShapeDtypeStruct((bsz, n_rows, di), BF16),
        scratch_shapes=[
            pltpu.VMEM((ng, ns, gw), F32),
            pltpu.VMEM((ng, 8, ew), F32),
            pltpu.VMEM((16, ew), F32),
            pltpu.VMEM((cw - 1, ch, ch), BF16),
            pltpu.VMEM((2, ch, ch), BF16),
        ],
        compiler_params=_params("arbitrary", "arbitrary", "arbitrary"),
        name="ssd",
    )(main3, main3, main3, main3, tail_t, cwx, cwb, cwc, cbx, cbb, cbc,
      dt_bias.reshape(ng, hpg, 1), a_log.reshape(ng, hpg, 1),
      jnp.repeat(d_skip, cfg.ssm_head_dim).reshape(1, di), norm_w.reshape(1, di))


def _merge_kernel(attn_ref, ssm_ref, ga_ref, gs_ref, wa_ref, ws_ref, o_ref):
    pa = jnp.dot(attn_ref[...], wa_ref[...], preferred_element_type=F32)
    ps = jnp.dot(ssm_ref[...], ws_ref[...], preferred_element_type=F32)
    ga = jax.nn.sigmoid(ga_ref[...].astype(F32))
    gs = jax.nn.sigmoid(gs_ref[...].astype(F32))
    o_ref[...] = (ga * pa + gs * ps).astype(o_ref.dtype)


def _merge(cfg, attn, ssm, main, w_a, w_s, bsz, n_rows):
    ch, d = cfg.chunk, cfg.d_model
    nq = n_rows // ch - 1
    tn = _pick(d, (1024, 512, 256))
    main3 = main.reshape(bsz, n_rows, -1)
    g_off = cfg.qw + 2 * cfg.kvw + cfg.iw + 2 * cfg.d_inner + 2 * cfg.bcw
    assert g_off % tn == 0
    return pl.pallas_call(
        _merge_kernel,
        grid=(d // tn, bsz, nq),
        in_specs=[
            pl.BlockSpec((None, ch, cfg.qw), lambda j, b, i: (b, i, 0)),
            pl.BlockSpec((None, ch, cfg.d_inner), lambda j, b, i: (b, i + 1, 0)),
            pl.BlockSpec((None, ch, tn), lambda j, b, i: (b, i + 1, g_off // tn + j)),
            pl.BlockSpec((None, ch, tn), lambda j, b, i: (b, i + 1, (g_off + d) // tn + j)),
            pl.BlockSpec((cfg.qw, tn), lambda j, b, i: (0, j)),
            pl.BlockSpec((cfg.d_inner, tn), lambda j, b, i: (0, j)),
        ],
        out_specs=pl.BlockSpec((None, ch, tn), lambda j, b, i: (b, i, j)),
        out_shape=jax.ShapeDtypeStruct((bsz, nq * ch, d), BF16),
        compiler_params=_params("parallel", "parallel", "parallel"),
        name="merge",
    )(attn, ssm, main3, main3, w_a, w_s)


def _outproj_kernel(eps, m_ref, h_ref, w_ref, nw_ref, h2_ref, hn_ref, hnt_ref):
    h2 = h_ref[...] + jnp.dot(m_ref[...], w_ref[...], preferred_element_type=F32)
    h2_ref[...] = h2
    ms = jnp.mean(h2 * h2, axis=-1, keepdims=True)
    hn = h2 * lax.rsqrt(ms + eps) * nw_ref[...]
    hn_ref[...] = hn.astype(hn_ref.dtype)
    hnt_ref[...] = hn.T.astype(hnt_ref.dtype)


def _outproj(cfg, merged, x, w_out, norm_w, bsz, n_rows):
    ch, d = cfg.chunk, cfg.d_model
    nq = n_rows // ch - 1
    return pl.pallas_call(
        functools.partial(_outproj_kernel, cfg.eps),
        grid=(bsz, nq),
        in_specs=[
            pl.BlockSpec((None, ch, d), lambda b, i: (b, i, 0)),
            pl.BlockSpec((None, ch, d), lambda b, i: (b, i, 0)),
            pl.BlockSpec((d, d), lambda b, i: (0, 0)),
            pl.BlockSpec((1, d), lambda b, i: (0, 0)),
        ],
        out_specs=[
            pl.BlockSpec((None, ch, d), lambda b, i: (b, i, 0)),
            pl.BlockSpec((None, ch, d), lambda b, i: (b, i, 0)),
            pl.BlockSpec((d, ch), lambda b, i: (0, b * nq + i)),
        ],
        out_shape=[
            jax.ShapeDtypeStruct((bsz, nq * ch, d), F32),
            jax.ShapeDtypeStruct((bsz, nq * ch, d), BF16),
            jax.ShapeDtypeStruct((d, bsz * nq * ch), BF16),
        ],
        compiler_params=_params("parallel", "parallel"),
        name="outproj",
    )(merged, x, w_out, norm_w.reshape(1, d))


def _topk_desc(s, k):
    out = []
    for _ in range(k):
        m = jnp.max(s, axis=0, keepdims=True)
        out.append(m)
        s = jnp.where(s >= m, -jnp.inf, s)
    return out


def _oddeven_merge_sort(n):
    pairs = []
    p = 1
    while p < n:
        k = p
        while k >= 1:
            for j in range(k % p, n - k, 2 * k):
                for i in range(min(k, n - j - k)):
                    if (i + j) // (2 * p) == (i + j + k) // (2 * p):
                        pairs.append((i + j, i + j + k))
            k //= 2
        p *= 2
    return pairs


def _top16_sorted(s):
    n = 16
    v = [s[r * 8:(r + 1) * 8, :] for r in range(n)]
    for i, j in _oddeven_merge_sort(n):
        v[i], v[j] = jnp.maximum(v[i], v[j]), jnp.minimum(v[i], v[j])
    for shift in (4, 2, 1):
        v = [jnp.maximum(v[r], pltpu.roll(v[n - 1 - r], shift, 0)) for r in range(n)]
        span = n // 2
        while span >= 1:
            for lo in range(n):
                if (lo // span) % 2 == 0:
                    hi = lo + span
                    v[lo], v[hi] = jnp.maximum(v[lo], v[hi]), jnp.minimum(v[lo], v[hi])
            span //= 2
    return v


def _route_kernel(cfg, x_ref, wq_ref, keys_ref, thr_ref, e0_ref, s1_ref, e1_ref):
    nk, k = cfg.n_keys, cfg.peer_topk
    half = cfg.peer_key_dim // 2
    q = jnp.dot(x_ref[...], wq_ref[...], preferred_element_type=F32)
    for h in range(cfg.peer_heads):
        st = []
        for cidx in range(2):
            off = (2 * h + cidx) * half
            qc = q[:, off:off + half].astype(BF16)
            st.append(lax.dot_general(keys_ref[cidx], qc, NT, preferred_element_type=F32))
        t0 = [t[0:1, :] for t in _top16_sorted(st[0])]
        t1 = [t[0:1, :] for t in _top16_sorted(st[1])]
        t1m = jnp.concatenate(t1, axis=0)
        cands = [a + t1m for a in t0]
        short = [cands[r][0:k // (r + 1), :] for r in range(k)]
        n_short = sum(k // (r + 1) for r in range(k))
        short.append(jnp.full((-n_short % 8, t1m.shape[1]), -jnp.inf, F32))
        best = _topk_desc(jnp.concatenate(short, axis=0), k)
        z = sum(jnp.exp(b - best[0]) for b in best)
        thr_i = jnp.full(st[0].shape, jnp.inf, F32)
        for r in range(k):
            lim = jnp.min(jnp.where(cands[r] >= best[k - 1], t1m, jnp.inf), axis=0, keepdims=True)
            thr_i = jnp.where(st[0] == t0[r], lim, thr_i)
        thr_ref[h] = thr_i
        e0_ref[h] = jnp.exp(st[0] - t0[0])
        s1_ref[h] = st[1]
        e1_ref[h] = jnp.exp(st[1] - t1[0]) / z


def _route(cfg, hn2, wq, keys):
    n, d = hn2.shape
    tm = 256
    nh, nk = cfg.peer_heads, cfg.n_keys
    qd = nh * cfg.peer_key_dim
    assert nk == LANES and cfg.peer_key_dim // 2 == LANES and cfg.peer_topk == 16
    big = pl.BlockSpec((nh, nk, tm), lambda i: (0, 0, i))
    shp = jax.ShapeDtypeStruct((nh, nk, n), F32)
    return pl.pallas_call(
        functools.partial(_route_kernel, cfg),
        grid=(n // tm,),
        in_specs=[
            pl.BlockSpec((tm, d), lambda i: (i, 0)),
            pl.BlockSpec((d, qd), lambda i: (0, 0)),
            pl.BlockSpec((2, nk, cfg.peer_key_dim // 2), lambda i: (0, 0, 0)),
        ],
        out_specs=[big, big, big, big],
        out_shape=[shp, shp, shp, shp],
        compiler_params=_params("parallel"),
        name="peer_route",
    )(hn2, wq, keys)


def _gelu(x):
    return 0.5 * x * (1.0 + lax.erf(x * (2.0 ** -0.5)))


def _gate_tile(cfg, i, lanes, thr_ref, e0_ref, s1_ref, e1_ref):
    w = None
    for h in range(cfg.peer_heads):
        sel = s1_ref[h, :, lanes] >= thr_ref[h, i:i + 1, lanes]
        term = jnp.where(sel, e1_ref[h, :, lanes], 0.0) * e0_ref[h, i:i + 1, lanes]
        w = term if w is None else w + term
    return w


MXU_ROWS = 512


def _peer_kernel(cfg, ic, n_e, xt_ref, u_ref, vt_ref, thr_ref, e0_ref, s1_ref, e1_ref, o_ref, a_ref, hb_ref):
    nk = cfg.n_keys
    d, tm = o_ref.shape
    ec = ic * nk
    s = pl.program_id(0)
    e_down = jnp.maximum(s - 2, 0) % n_e
    slot = s % 2

    @pl.when(s == 0)
    def _():
        a_ref[1] = jnp.zeros(a_ref.shape[1:], F32)
        hb_ref[...] = jnp.zeros(hb_ref.shape, BF16)

    @pl.when(e_down == 0)
    def _():
        o_ref[...] = jnp.zeros(o_ref.shape, F32)

    up_rows, down_rows = min(MXU_ROWS, ec), min(MXU_ROWS, d)

    def up(m0):
        rows = slice(m0, m0 + up_rows)
        a_ref[slot, rows, :] = jnp.dot(u_ref[rows, :], xt_ref[...], preferred_element_type=F32)

    def down(m0):
        rows = slice(m0, m0 + down_rows)
        o_ref[rows, :] += jnp.dot(vt_ref[rows, :], hb_ref[slot], preferred_element_type=F32)

    def gate(i, c):
        rows, lanes = slice(i * nk, (i + 1) * nk), slice(c * LANES, (c + 1) * LANES)
        w = _gate_tile(cfg, i, lanes, thr_ref, e0_ref, s1_ref, e1_ref)
        hb_ref[1 - slot, rows, lanes] = (w * _gelu(a_ref[1 - slot, rows, lanes])).astype(BF16)

    mxu = [functools.partial(up, m0) for m0 in range(0, ec, up_rows)]
    mxu += [functools.partial(down, m0) for m0 in range(0, d, down_rows)]
    vec = [functools.partial(gate, i, c) for i in range(ic) for c in range(tm // LANES)]
    done = 0
    for k, piece in enumerate(mxu):
        piece()
        upto = (k + 1) * len(vec) // len(mxu)
        for tile in vec[done:upto]:
            tile()
        done = upto


def _peer(cfg, hn2t, u_b, v, routes):
    d, n = hn2t.shape
    nh, nk = cfg.peer_heads, cfg.n_keys
    tm = _pick(n, (512, 256))
    ic = 8
    ec = ic * nk
    n_e = nk // ic
    vt_b = v.reshape(n_e, ec, d).transpose(0, 2, 1).astype(BF16)
    n_steps = (n // tm) * n_e
    assert ec % min(MXU_ROWS, ec) == 0 and d % min(MXU_ROWS, d) == 0
    back = lambda s, k: jnp.clip(s - k, 0, n_steps - 1)
    chunked = pl.BlockSpec((nh, ic, tm), lambda s: (0, back(s, 1) % n_e, back(s, 1) // n_e))
    full = pl.BlockSpec((nh, nk, tm), lambda s: (0, 0, back(s, 1) // n_e))
    return pl.pallas_call(
        functools.partial(_peer_kernel, cfg, ic, n_e),
        grid=(n_steps + 2,),
        in_specs=[
            pl.BlockSpec((d, tm), lambda s: (0, back(s, 0) // n_e)),
            pl.BlockSpec((ec, d), lambda s: (back(s, 0) % n_e, 0)),
            pl.BlockSpec((None, d, ec), lambda s: (back(s, 2) % n_e, 0, 0)),
            chunked, chunked, full, full,
        ],
        out_specs=pl.BlockSpec((d, tm), lambda s: (0, back(s, 2) // n_e)),
        out_shape=jax.ShapeDtypeStruct((d, n), F32),
        scratch_shapes=[pltpu.VMEM((2, ec, tm), F32), pltpu.VMEM((2, ec, tm), BF16)],
        compiler_params=_params("arbitrary"),
        name="peer_experts",
    )(hn2t, u_b, vt_b, *routes)


def _final_kernel(eps, h2_ref, pt_ref, nw_ref, o_ref):
    h = h2_ref[...] + pt_ref[...].T
    ms = jnp.mean(h * h, axis=-1, keepdims=True)
    o_ref[...] = h * lax.rsqrt(ms + eps) * nw_ref[...]


def _final(cfg, h2, peer_t, norm_w):
    n, d = h2.shape
    tm = 256
    return pl.pallas_call(
        functools.partial(_final_kernel, cfg.eps),
        grid=(n // tm,),
        in_specs=[
            pl.BlockSpec((tm, d), lambda i: (i, 0)),
            pl.BlockSpec((d, tm), lambda i: (0, i)),
            pl.BlockSpec((1, d), lambda i: (0, 0)),
        ],
        out_specs=pl.BlockSpec((tm, d), lambda i: (i, 0)),
        out_shape=jax.ShapeDtypeStruct((n, d), F32),
        compiler_params=_params("parallel"),
        name="final_norm",
    )(h2, peer_t, norm_w.reshape(1, d))


def _layer(cfg, x, meta_tokens, norm_mix_w, w_in, conv_w, conv_b, dt_bias, a_log, d_skip, ssm_norm_w,
           w_branch_attn, w_branch_ssm, w_out, norm_ffn_w, peer_w_query, peer_sub_keys, peer_u, peer_v,
           norm_final_w):
    bsz, seq, d = x.shape
    ch = cfg.chunk
    assert seq % ch == 0 and norm_mix_w.shape[0] == 1
    n_rows = ch + seq
    topk = min(cfg.topk_max, seq // 4)
    w_t = w_in[0].T
    o_iq = cfg.qw + 2 * cfg.kvw
    o_ik = o_iq + cfg.iw
    o_iw = o_ik + cfg.idx_dim
    o_z = o_iw + cfg.idx_heads
    o_dt = o_z + 2 * cfg.d_inner + 2 * cfg.bcw
    o_g = o_dt + cfg.ssm_heads
    segments = ((0, o_ik), (o_z, o_dt - o_z), (o_g, w_t.shape[0] - o_g))
    zpad = lambda n: jnp.zeros((n, d), w_t.dtype)
    w_tail_t = jnp.concatenate([
        w_t[o_ik:o_z], zpad(TAIL_DT - cfg.idx_dim - cfg.idx_heads),
        w_t[o_dt:o_g], zpad(TAIL_W - TAIL_DT - cfg.ssm_heads)], axis=0)

    hn = _prenorm(cfg, x, meta_tokens, norm_mix_w[0])
    main, tail = _inproj(cfg, hn.reshape(bsz * n_rows, d), w_t, w_tail_t, segments)
    qk_r, qi_r, ki2, vt, tail_t = _rope(cfg, main, tail, _rope_tables(cfg, n_rows), bsz, n_rows)
    attn = _dsa(cfg, topk, qk_r, qi_r, ki2, vt, tail_t, bsz, n_rows)
    ssm = _ssd(cfg, main, tail_t, conv_w[0], conv_b[0], dt_bias[0], a_log[0], d_skip[0], ssm_norm_w[0], bsz, n_rows)
    merged = _merge(cfg, attn, ssm, main, w_branch_attn[0].astype(BF16), w_branch_ssm[0].astype(BF16), bsz, n_rows)
    h2, hn2, hn2t = _outproj(cfg, merged, x, w_out[0].astype(BF16), norm_ffn_w[0], bsz, n_rows)
    h2 = h2.reshape(bsz * seq, d)
    hn2 = hn2.reshape(bsz * seq, d)
    routes = _route(cfg, hn2, peer_w_query[0].astype(BF16), peer_sub_keys[0].astype(BF16))
    peer_t = _peer(cfg, hn2t, peer_u[0].astype(BF16), peer_v[0], routes)
    out = _final(cfg, h2, peer_t, norm_final_w)
    return out.reshape(bsz, seq, d)


def kernel(x, meta_tokens, norm_mix_w, w_in, conv_w, conv_b, dt_bias, a_log, d_skip, ssm_norm_w, w_branch_attn,
           w_branch_ssm, w_out, norm_ffn_w, peer_w_query, peer_sub_keys, peer_u, peer_v, norm_final_w):
    return _layer(Cfg(), x, meta_tokens, norm_mix_w, w_in, conv_w, conv_b, dt_bias, a_log, d_skip, ssm_norm_w,
                  w_branch_attn, w_branch_ssm, w_out, norm_ffn_w, peer_w_query, peer_sub_keys, peer_u, peer_v,
                  norm_final_w)
```

```python
import functools
import math
from typing import NamedTuple

import jax
import jax.numpy as jnp
from jax import lax
from jax.experimental import pallas as pl
from jax.experimental.pallas import tpu as pltpu

F32 = jnp.float32
BF16 = jnp.bfloat16
LANES = 128
LOG2E = math.log2(math.e)
NEG_BIG = -1e30
VMEM_LIMIT = 56 * 1024 * 1024


class Cfg(NamedTuple):
    d_model: int = 2048
    n_meta: int = 16
    attn_heads: int = 16
    kv_heads: int = 4
    head_dim: int = 128
    idx_heads: int = 16
    idx_dim: int = 64
    topk_max: int = 256
    rope_theta: float = 10000.0
    d_inner: int = 4096
    ssm_head_dim: int = 64
    ssm_groups: int = 8
    d_state: int = 128
    conv_width: int = 4
    chunk: int = 256
    peer_heads: int = 8
    n_keys: int = 128
    peer_key_dim: int = 256
    peer_topk: int = 16
    eps: float = 1e-6

    @property
    def lead(self):
        return self.chunk - self.n_meta

    @property
    def qw(self):
        return self.attn_heads * self.head_dim

    @property
    def kvw(self):
        return self.kv_heads * self.head_dim

    @property
    def iw(self):
        return self.idx_heads * self.idx_dim

    @property
    def ssm_heads(self):
        return self.d_inner // self.ssm_head_dim

    @property
    def hpg(self):
        return self.ssm_heads // self.ssm_groups

    @property
    def gw(self):
        return self.d_inner // self.ssm_groups

    @property
    def bcw(self):
        return self.ssm_groups * self.d_state


NT = (((1,), (1,)), ((), ()))
TAIL_W = 256
TAIL_DT = 128


def _params(*sem):
    return pltpu.CompilerParams(dimension_semantics=sem, vmem_limit_bytes=VMEM_LIMIT)


def _rms(x, w, eps):
    ms = jnp.mean(x * x, axis=-1, keepdims=True)
    return x * lax.rsqrt(ms + eps) * w


def _prenorm_kernel(cfg, x_ref, meta_ref, nw_ref, o_ref):
    c = pl.program_id(1)

    @pl.when(c == 0)
    def _():
        o_ref[0:cfg.lead, :] = jnp.zeros((cfg.lead, o_ref.shape[1]), o_ref.dtype)
        o_ref[cfg.lead:cfg.chunk, :] = _rms(meta_ref[...], nw_ref[...], cfg.eps).astype(o_ref.dtype)

    @pl.when(c > 0)
    def _():
        o_ref[...] = _rms(x_ref[...], nw_ref[...], cfg.eps).astype(o_ref.dtype)


def _prenorm(cfg, x, meta_tokens, norm_w):
    bsz, seq, d = x.shape
    ch = cfg.chunk
    nc = seq // ch + 1
    return pl.pallas_call(
        functools.partial(_prenorm_kernel, cfg),
        grid=(bsz, nc),
        in_specs=[
            pl.BlockSpec((None, ch, d), lambda b, c: (b, jnp.maximum(c - 1, 0), 0)),
            pl.BlockSpec((cfg.n_meta, d), lambda b, c: (0, 0)),
            pl.BlockSpec((1, d), lambda b, c: (0, 0)),
        ],
        out_specs=pl.BlockSpec((None, ch, d), lambda b, c: (b, c, 0)),
        out_shape=jax.ShapeDtypeStruct((bsz, nc * ch, d), BF16),
        compiler_params=_params("parallel", "parallel"),
        name="prenorm",
    )(x, meta_tokens.astype(x.dtype), norm_w.reshape(1, d))


def _inproj_kernel(hn_ref, w_ref, wt_ref, main_ref, tail_ref):
    @pl.when(pl.program_id(1) == 0)
    def _():
        tail_ref[...] = lax.dot_general(hn_ref[...], wt_ref[...].astype(BF16), NT, preferred_element_type=F32)

    w = w_ref[...].astype(BF16)
    main_ref[...] = lax.dot_general(hn_ref[...], w, NT, preferred_element_type=F32).astype(main_ref.dtype)


def _pick(n, cands):
    for c in cands:
        if n % c == 0:
            return c
    raise ValueError(f"no tile for {n}")


def _inproj(cfg, hn2d, w_t, w_tail_t, segments):
    m, d = hn2d.shape
    nm = sum(n for _, n in segments)
    tm = _pick(m, (1088, 512, 256))
    tn = _pick(math.gcd(*[n for _, n in segments]), (1024, 512, 256))

    sub = 8
    assert all(start % sub == 0 and n % tn == 0 for start, n in segments)

    def w_row(j):
        row = j * (tn // sub)
        done = 0
        for start, n in segments:
            shift = (start - done) // sub
            row = row + jnp.where(j >= done // tn, shift, 0) - jnp.where(j >= (done + n) // tn, shift, 0)
            done += n
        return row * sub

    return pl.pallas_call(
        _inproj_kernel,
        grid=(m // tm, nm // tn),
        in_specs=[
            pl.BlockSpec((tm, d), lambda i, j: (i, 0)),
            pl.BlockSpec((pl.Element(tn), pl.Element(d)), lambda i, j: (w_row(j), 0)),
            pl.BlockSpec((TAIL_W, d), lambda i, j: (0, 0)),
        ],
        out_specs=[
            pl.BlockSpec((tm, tn), lambda i, j: (i, j)),
            pl.BlockSpec((tm, TAIL_W), lambda i, j: (i, 0)),
        ],
        out_shape=[
            jax.ShapeDtypeStruct((m, nm), BF16),
            jax.ShapeDtypeStruct((m, TAIL_W), F32),
        ],
        compiler_params=_params("parallel", "arbitrary"),
        name="inproj",
    )(hn2d, w_t, w_tail_t)


def _rope_kernel(n_q, n_qk, n_ipair, idx_dim, q_scale, qk_ref, v_ref, qi_ref, tail_ref, cs_ref, sn_ref, csi_ref,
                 sni_ref, qk_out, qi_out, ki_out, vt_out, tt_out):
    cs = cs_ref[...]
    sn = sn_ref[...]
    for hd in range(n_qk):
        sl = slice(hd * LANES, (hd + 1) * LANES)
        x = qk_ref[:, sl].astype(F32)
        y = x * cs + pltpu.roll(x, LANES // 2, 1) * sn
        if hd < n_q:
            y = y * q_scale
        qk_out[:, sl] = y.astype(BF16)
    csi = csi_ref[...]
    sni = sni_ref[...]
    half = idx_dim // 2
    lane = lax.broadcasted_iota(jnp.int32, csi.shape, 1)
    first = (lane % idx_dim) < half

    def rope_idx(x):
        rot = jnp.where(first, pltpu.roll(x, LANES - half, 1), pltpu.roll(x, half, 1))
        return x * csi + rot * sni

    for p in range(n_ipair):
        sl = slice(p * LANES, (p + 1) * LANES)
        qi_out[:, sl] = rope_idx(qi_ref[:, sl].astype(F32)).astype(BF16)
    kt = rope_idx(tail_ref[:, 0:LANES])
    klo = jnp.where(lane < idx_dim, kt, 0.0)
    ki_out[:, 0:LANES] = klo.astype(BF16)
    ki_out[:, LANES:2 * LANES] = pltpu.roll(klo, idx_dim, 1).astype(BF16)
    vt_out[...] = v_ref[...].astype(F32).T.astype(BF16)
    tt_out[...] = tail_ref[...].T


def _rope(cfg, main, tail, tabs, bsz, n_rows):
    m = main.shape[0]
    tm = 256
    nb = n_rows // tm
    n_q, n_kv = cfg.attn_heads, cfg.kv_heads
    qkw = cfg.qw + cfg.kvw
    assert cfg.head_dim == LANES and 2 * cfg.idx_dim == LANES
    qi_off = cfg.qw + 2 * cfg.kvw
    assert qi_off % cfg.iw == 0 and qkw % cfg.kvw == 0
    cs, sn, csi, sni = tabs
    tab_spec = pl.BlockSpec((tm, LANES), lambda i: (i % nb, 0))
    return pl.pallas_call(
        functools.partial(_rope_kernel, n_q, n_q + n_kv, cfg.iw // LANES, cfg.idx_dim, cfg.head_dim ** -0.5 * LOG2E),
        grid=(m // tm,),
        in_specs=[
            pl.BlockSpec((tm, qkw), lambda i: (i, 0)),
            pl.BlockSpec((tm, cfg.kvw), lambda i: (i, qkw // cfg.kvw)),
            pl.BlockSpec((tm, cfg.iw), lambda i: (i, qi_off // cfg.iw)),
            pl.BlockSpec((tm, TAIL_W), lambda i: (i, 0)),
            tab_spec, tab_spec, tab_spec, tab_spec,
        ],
        out_specs=[
            pl.BlockSpec((tm, qkw), lambda i: (i, 0)),
            pl.BlockSpec((tm, cfg.iw), lambda i: (i, 0)),
            pl.BlockSpec((tm, 2 * LANES), lambda i: (i, 0)),
            pl.BlockSpec((None, cfg.kvw, tm), lambda i: (i // nb, 0, i % nb)),
            pl.BlockSpec((None, TAIL_W, tm), lambda i: (i // nb, 0, i % nb)),
        ],
        out_shape=[
            jax.ShapeDtypeStruct((m, qkw), BF16),
            jax.ShapeDtypeStruct((m, cfg.iw), BF16),
            jax.ShapeDtypeStruct((m, 2 * LANES), BF16),
            jax.ShapeDtypeStruct((bsz, cfg.kvw, n_rows), BF16),
            jax.ShapeDtypeStruct((bsz, TAIL_W, n_rows), F32),
        ],
        compiler_params=_params("parallel"),
        name="rope",
    )(main, main, main, tail, cs, sn, csi, sni)


def _rope_tables(cfg, n_rows):
    pos = (jnp.arange(n_rows) - cfg.lead).astype(F32)

    def tab(dim):
        half = dim // 2
        inv = cfg.rope_theta ** (-jnp.arange(half, dtype=F32) / half)
        ang = pos[:, None] * inv[None, :]
        c, s = jnp.cos(ang), jnp.sin(ang)
        reps = LANES // dim
        return jnp.tile(jnp.concatenate([c, c], -1), (1, reps)), jnp.tile(jnp.concatenate([-s, s], -1), (1, reps))

    cs, sn = tab(cfg.head_dim)
    csi, sni = tab(cfg.idx_dim)
    return cs, sn, csi, sni


NEG_INF_KEY = -2139095041
CNT_CHAINS = 4


def _dsa_kernel(cfg, topk, qi_ref, wt_ref, ki_ref, q_ref, k_ref, vt_ref, o_ref, key_ref, qs_ref, acc_ref, m_ref, l_ref,
                al_ref, s_ref, p_ref):
    ch = cfg.chunk
    c = pl.program_id(1) + 1
    n_kc = c + 1
    rep = cfg.attn_heads // cfg.kv_heads
    hd = cfg.head_dim
    row0 = c * ch
    int_min = jnp.int32(-2 ** 31)
    sub8 = ch // 8

    q_row = row0 + lax.broadcasted_iota(jnp.int32, (ch, ch), 1)
    k_sub = lax.broadcasted_iota(jnp.int32, (ch, ch), 0)

    def admissible(koff):
        k_row = koff + k_sub
        return (k_row >= cfg.lead) & (k_row <= q_row)

    w_all = wt_ref[...] * (cfg.idx_heads ** -0.5 * cfg.idx_dim ** -0.5)

    def score_rows(koff, nrow):
        k_lo = ki_ref[pl.ds(koff, nrow), 0:LANES]
        k_hi = ki_ref[pl.ds(koff, nrow), LANES:2 * LANES]
        acc = jnp.zeros((nrow, ch), F32)
        for p in range(cfg.idx_heads // 2):
            qp = qi_ref[:, p * LANES:(p + 1) * LANES]
            for half, kk in ((0, k_lo), (1, k_hi)):
                h = 2 * p + half
                lg = lax.dot_general(kk, qp, NT, preferred_element_type=F32)
                acc = acc + jnp.maximum(lg, 0.0) * w_all[h:h + 1, :]
        k_row = koff + lax.broadcasted_iota(jnp.int32, (nrow, ch), 0)
        adm = (k_row >= cfg.lead) & (k_row <= row0 + lax.broadcasted_iota(jnp.int32, (nrow, ch), 1))
        bits = lax.bitcast_convert_type(jnp.where(adm, acc, -jnp.inf), jnp.int32)
        key_ref[pl.ds(koff, nrow), :] = bits ^ ((bits >> 31) & jnp.int32(0x7FFFFFFF))

    def score_pair(kc2, carry):
        score_rows(pl.multiple_of(kc2 * (2 * ch), 2 * ch), 2 * ch)
        return carry

    lax.fori_loop(0, n_kc // 2, score_pair, 0)

    @pl.when(n_kc % 2 == 1)
    def _():
        score_rows(pl.multiple_of((n_kc - 1) * ch, ch), ch)

    key_ref[pl.ds(pl.multiple_of(n_kc * ch, ch), ch), :] = jnp.full((ch, ch), NEG_INF_KEY, jnp.int32)

    def count(pred_of_keys):
        def cnt_chunk(kc2, cnt):
            koff = pl.multiple_of(kc2 * (2 * ch), 2 * ch)
            ind = jnp.where(pred_of_keys(key_ref[pl.ds(koff, 2 * ch), :]), 1.0, 0.0)
            return cnt + jnp.sum(ind.reshape(2 * sub8 // CNT_CHAINS, CNT_CHAINS, 8, ch), axis=0)

        cnt = lax.fori_loop(0, (n_kc + 1) // 2, cnt_chunk, jnp.zeros((CNT_CHAINS, 8, ch), F32))
        return jnp.sum(jnp.sum(cnt, axis=0), axis=0, keepdims=True)

    kf = float(topk)

    def bit_step(it, state):
        tau_u, n_ge = state
        cand_u = tau_u | lax.shift_left(jnp.int32(1), jnp.int32(31) - it)
        cand = cand_u ^ int_min
        tot = count(lambda kv: kv >= cand)
        take = tot >= kf
        return jnp.where(take, cand_u, tau_u), jnp.where(take, tot, n_ge)

    tau_u, n_ge = lax.fori_loop(0, 32, bit_step, (jnp.zeros((1, ch), jnp.int32), jnp.full((1, ch), 2.0 * kf, F32)))
    tau = tau_u ^ int_min

    tied = jnp.max(jnp.where((n_ge > kf) & (tau != NEG_INF_KEY), 1.0, 0.0)) > 0.0

    @pl.when(jnp.logical_not(tied))
    def _():
        def bias_chunk(kc, carry):
            koff = pl.multiple_of(kc * ch, ch)
            sel = (key_ref[pl.ds(koff, ch), :] >= tau) & admissible(koff)
            key_ref[pl.ds(koff, ch), :] = lax.bitcast_convert_type(jnp.where(sel, 0.0, NEG_BIG).astype(F32), jnp.int32)
            return carry

        lax.fori_loop(0, n_kc, bias_chunk, 0)

    @pl.when(tied)
    def _():
        need = kf - count(lambda kv: kv > tau)
        below = (k_sub >= q_row - row0).astype(BF16)

        def bias_chunk(kc, seen):
            koff = pl.multiple_of(kc * ch, ch)
            kv = key_ref[pl.ds(koff, ch), :]
            eq = kv == tau
            rank = seen + jnp.dot(below, jnp.where(eq, 1.0, 0.0).astype(BF16), preferred_element_type=F32)
            sel = ((kv > tau) | (eq & (rank <= need))) & admissible(koff)
            key_ref[pl.ds(koff, ch), :] = lax.bitcast_convert_type(jnp.where(sel, 0.0, NEG_BIG).astype(F32), jnp.int32)
            return rank[ch - 1:ch, :]

        lax.fori_loop(0, n_kc, bias_chunk, jnp.zeros((1, ch), F32))

    for g in range(cfg.kv_heads):
        for r in range(rep):
            h = g * rep + r
            qs_ref[g, r * ch:(r + 1) * ch, :] = q_ref[:, h * hd:(h + 1) * hd]
    acc_ref[...] = jnp.zeros(acc_ref.shape, F32)
    m_ref[...] = jnp.full(m_ref.shape, NEG_BIG, F32)
    l_ref[...] = jnp.zeros(l_ref.shape, F32)

    def attn_chunk(kc, carry):
        koff = pl.multiple_of(kc * ch, ch)

        def scores(g):
            kg = k_ref[pl.ds(koff, ch), g * hd:(g + 1) * hd]
            s_ref[g % 2] = lax.dot_general(kg, qs_ref[g], NT, preferred_element_type=F32)

        def softmax_pv(g):
            for t in range(rep * ch // LANES):
                lanes = slice(t * LANES, (t + 1) * LANES)
                qlanes = slice((t * LANES) % ch, (t * LANES) % ch + LANES)
                s = s_ref[g % 2, :, lanes] + lax.bitcast_convert_type(key_ref[pl.ds(koff, ch), qlanes], F32)
                m_old = m_ref[g, 0:1, lanes]
                m_new = jnp.maximum(m_old, jnp.max(s, axis=0, keepdims=True))
                alpha = jnp.exp2(m_old - m_new)
                p = jnp.exp2(s - m_new)
                l_ref[g, 0:1, lanes] = alpha * l_ref[g, 0:1, lanes] + jnp.sum(p, axis=0, keepdims=True)
                m_ref[g, 0:1, lanes] = m_new
                al_ref[g % 2, 0:1, lanes] = alpha
                p_ref[g % 2, :, lanes] = p.astype(BF16)
            vtg = vt_ref[g * hd:(g + 1) * hd, pl.ds(koff, ch)]
            acc_ref[g] = al_ref[g % 2, 0:1, :] * acc_ref[g] + jnp.dot(vtg, p_ref[g % 2], preferred_element_type=F32)

        scores(0)
        for g in range(cfg.kv_heads):
            if g + 1 < cfg.kv_heads:
                scores(g + 1)
            softmax_pv(g)
        return carry

    lax.fori_loop(0, n_kc, attn_chunk, 0)

    for g in range(cfg.kv_heads):
        og = acc_ref[g] / l_ref[g, 0:1, :]
        for r in range(rep):
            h = g * rep + r
            o_ref[:, h * hd:(h + 1) * hd] = og[:, r * ch:(r + 1) * ch].T.astype(o_ref.dtype)


def _dsa(cfg, topk, qk_r, qi_r, ki2, vt, tail_t, bsz, n_rows):
    ch = cfg.chunk
    nq = n_rows // ch - 1
    rep = cfg.attn_heads // cfg.kv_heads
    qk3 = qk_r.reshape(bsz, n_rows, -1)
    qi3 = qi_r.reshape(bsz, n_rows, -1)
    ki3 = ki2.reshape(bsz, n_rows, -1)
    assert cfg.qw % cfg.kvw == 0 and cfg.idx_dim % cfg.idx_heads == 0 and cfg.idx_heads % 8 == 0
    k_blk = cfg.qw // cfg.kvw
    return pl.pallas_call(
        functools.partial(_dsa_kernel, cfg, topk),
        grid=(bsz, nq),
        in_specs=[
            pl.BlockSpec((None, ch, cfg.iw), lambda b, i: (b, i + 1, 0)),
            pl.BlockSpec((None, cfg.idx_heads, ch), lambda b, i: (b, cfg.idx_dim // cfg.idx_heads, i + 1)),
            pl.BlockSpec((None, n_rows, 2 * LANES), lambda b, i: (b, 0, 0)),
            pl.BlockSpec((None, ch, cfg.qw), lambda b, i: (b, i + 1, 0)),
            pl.BlockSpec((None, n_rows, cfg.kvw), lambda b, i: (b, 0, k_blk)),
            pl.BlockSpec((None, cfg.kvw, n_rows), lambda b, i: (b, 0, 0)),
        ],
        out_specs=pl.BlockSpec((None, ch, cfg.qw), lambda b, i: (b, i, 0)),
        out_shape=jax.ShapeDtypeStruct((bsz, nq * ch, cfg.qw), BF16),
        scratch_shapes=[
            pltpu.VMEM((n_rows + ch, ch), jnp.int32),
            pltpu.VMEM((cfg.kv_heads, rep * ch, cfg.head_dim), BF16),
            pltpu.VMEM((cfg.kv_heads, cfg.head_dim, rep * ch), F32),
            pltpu.VMEM((cfg.kv_heads, 8, rep * ch), F32),
            pltpu.VMEM((cfg.kv_heads, 8, rep * ch), F32),
            pltpu.VMEM((2, 8, rep * ch), F32),
            pltpu.VMEM((2, ch, rep * ch), F32),
            pltpu.VMEM((2, ch, rep * ch), BF16),
        ],
        compiler_params=_params("parallel", "arbitrary"),
        name="dsa",
    )(qi3, tail_t, ki3, qk3, qk3, vt)


def _split3(a):
    hi = a.astype(BF16)
    r1 = a - hi.astype(F32)
    mid = r1.astype(BF16)
    lo = (r1 - mid.astype(F32)).astype(BF16)
    return hi, mid, lo


def _dot3_nt(ones_lhs, a):
    return sum(lax.dot_general(ones_lhs, t, NT, preferred_element_type=F32) for t in _split3(a))


def _dot3_lhs(a, ones_rhs):
    return sum(jnp.dot(t, ones_rhs, preferred_element_type=F32) for t in _split3(a))


ROW_TILE = 128


def _silu(x):
    return x * jax.nn.sigmoid(x)


def _ssd_kernel(cfg, z_ref, x_ref, b_ref, c_ref, dtr_ref, cwx_ref, cwb_ref, cwc_ref, cbx_ref, cbb_ref, cbc_ref,
                dbr_ref, alr_ref, dsk_ref, nw_ref, y_ref, state_ref, carry_ref, head_ref, shift_ref, tri_ref):
    ch, gw, ns, hdim, hpg = cfg.chunk, cfg.gw, cfg.d_state, cfg.ssm_head_dim, cfg.hpg
    cwid = cfg.conv_width
    c = pl.program_id(1)
    g = pl.program_id(2)
    ew = gw + 2 * ns

    @pl.when(c == 0)
    def _():
        state_ref[g] = jnp.zeros((ns, gw), F32)
        carry_ref[g] = jnp.zeros((8, ew), F32)

    @pl.when((c == 0) & (g == 0))
    def _():
        ti = lax.broadcasted_iota(jnp.int32, (ch, ch), 0)
        sj = lax.broadcasted_iota(jnp.int32, (ch, ch), 1)
        for k in range(1, cwid):
            shift_ref[k - 1] = jnp.where(sj == ti - k, 1.0, 0.0).astype(BF16)
        tri_ref[0] = jnp.where(ti >= sj, 1.0, 0.0).astype(BF16)
        tri_ref[1] = jnp.where(ti <= sj, 1.0, 0.0).astype(BF16)

    cur = jnp.concatenate([x_ref[...], b_ref[...], c_ref[...]], axis=1)
    curf = cur.astype(F32)
    head_ref[0:8, :] = carry_ref[g]
    head_ref[8:16, :] = curf[0:8, :]
    carry_ref[g] = curf[ch - 8:ch, :]
    cw_all = jnp.concatenate([cwx_ref[...], cwb_ref[...], cwc_ref[...]], axis=1)
    cb_all = jnp.concatenate([cbx_ref[...], cbb_ref[...], cbc_ref[...]], axis=1)
    acc = curf * cw_all[cwid - 1:cwid, :] + cb_all
    for k in range(1, cwid):
        sh = jnp.dot(shift_ref[k - 1], cur, preferred_element_type=F32)
        sh = jnp.concatenate([head_ref[8 - k:16 - k, :], sh[8:, :]], axis=0)
        acc = acc + sh * cw_all[cwid - 1 - k:cwid - k, :]
    conv = _silu(acc)
    xc = conv[:, 0:gw]
    bc = conv[:, gw:gw + ns]
    cc = conv[:, gw + ns:ew]

    coli = lax.broadcasted_iota(jnp.int32, (hpg, ch), 1)
    is_pad_r = (coli < cfg.lead) & (c == 0)
    dt_r = jnp.where(is_pad_r, 0.0, jax.nn.softplus(dtr_ref[...] + dbr_ref[...]))
    a_r = dt_r * (-jnp.exp(alr_ref[...]) * LOG2E)
    li = lax.broadcasted_iota(jnp.int32, (ch, ch), 0)
    si = lax.broadcasted_iota(jnp.int32, (ch, ch), 1)
    causal = li >= si
    acs_c = _dot3_nt(tri_ref[0], a_r)
    acs_r = _dot3_lhs(a_r, tri_ref[1])
    a_end = acs_r[:, ch - 1:ch]
    wdec_r = dt_r * jnp.exp2(a_end - acs_r)

    cb = lax.dot_general(cc.astype(BF16), bc.astype(BF16), NT, preferred_element_type=F32)
    cb = jnp.where(causal, cb, 0.0)
    bt = bc.T
    xb = xc.astype(BF16)
    prev = state_ref[g]
    prev_b = prev.astype(BF16)
    lane_p = lax.broadcasted_iota(jnp.int32, (ch, 2 * hdim), 1)
    lane_s = lax.broadcasted_iota(jnp.int32, (ns, 2 * hdim), 1)

    y_pairs = []
    st_pairs = []
    for pr in range(hpg // 2):
        psl = slice(pr * 2 * hdim, (pr + 1) * 2 * hdim)
        x_pair = xb[:, psl]
        prev_pair = prev_b[:, psl]
        ys, sts = [], []
        for half in range(2):
            j = 2 * pr + half
            row = acs_r[j:j + 1, :]
            yh = []
            for r0 in range(0, ch, ROW_TILE):
                rs = slice(r0, r0 + ROW_TILE)
                col = acs_c[rs, j:j + 1]
                lmat = jnp.exp2(jnp.minimum(col - row, 0.0))
                mmat = (cb[rs, :] * lmat * dt_r[j:j + 1, :]).astype(BF16)
                ce = (cc[rs, :] * jnp.exp2(col)).astype(BF16)
                yh.append(jnp.dot(mmat, x_pair, preferred_element_type=F32)
                          + jnp.dot(ce, prev_pair, preferred_element_type=F32))
            ys.append(jnp.concatenate(yh, axis=0))
            btw = (bt * wdec_r[j:j + 1, :]).astype(BF16)
            sts.append(jnp.dot(btw, x_pair, preferred_element_type=F32))
        y_pairs.append(jnp.where(lane_p < hdim, ys[0], ys[1]))
        st_pairs.append(jnp.where(lane_s < hdim, sts[0], sts[1]))
    y = jnp.concatenate(y_pairs, axis=1)
    st = jnp.concatenate(st_pairs, axis=1)

    lane_g = lax.broadcasted_iota(jnp.int32, (1, gw), 1) // hdim
    cdec = jnp.zeros((1, gw), F32)
    for j in range(hpg):
        cdec = jnp.where(lane_g == j, jnp.exp2(a_end[j:j + 1, :]), cdec)
    state_ref[g] = prev * cdec + st

    y = y + xc * dsk_ref[...]
    y = y * _silu(z_ref[...].astype(F32))
    ms = jnp.mean(y * y, axis=-1, keepdims=True)
    y_ref[...] = (y * lax.rsqrt(ms + cfg.eps) * nw_ref[...]).astype(y_ref.dtype)


def _ssd(cfg, main, tail_t, conv_w, conv_b, dt_bias, a_log, d_skip, norm_w, bsz, n_rows):
    ch, gw, ns, hpg, ng = cfg.chunk, cfg.gw, cfg.d_state, cfg.hpg, cfg.ssm_groups
    nc = n_rows // ch
    di = cfg.d_inner
    main3 = main.reshape(bsz, n_rows, -1)
    z_off = cfg.qw + 2 * cfg.kvw + cfg.iw
    x_off = z_off + di
    b_off = x_off + di
    c_off = b_off + cfg.bcw
    assert z_off % gw == 0 and b_off % ns == 0 and hpg % 2 == 0 and 2 * cfg.ssm_head_dim == LANES
    cwx, cwb, cwc = conv_w[:, :di], conv_w[:, di:di + cfg.bcw], conv_w[:, di + cfg.bcw:]
    cb2 = conv_b.reshape(1, -1)
    cbx, cbb, cbc = cb2[:, :di], cb2[:, di:di + cfg.bcw], cb2[:, di + cfg.bcw:]
    ew = gw + 2 * ns
    cw = cfg.conv_width
    grp = lambda b, c, g: (b, c, g)
    return pl.pallas_call(
        functools.partial(_ssd_kernel, cfg),
        grid=(bsz, nc, ng),
        in_specs=[
            pl.BlockSpec((None, ch, gw), lambda b, c, g: (b, c, z_off // gw + g)),
            pl.BlockSpec((None, ch, gw), lambda b, c, g: (b, c, x_off // gw + g)),
            pl.BlockSpec((None, ch, ns), lambda b, c, g: (b, c, b_off // ns + g)),
            pl.BlockSpec((None, ch, ns), lambda b, c, g: (b, c, c_off // ns + g)),
            pl.BlockSpec((None, hpg, ch), lambda b, c, g: (b, TAIL_DT // hpg + g, c)),
            pl.BlockSpec((cw, gw), lambda b, c, g: (0, g)),
            pl.BlockSpec((cw, ns), lambda b, c, g: (0, g)),
            pl.BlockSpec((cw, ns), lambda b, c, g: (0, g)),
            pl.BlockSpec((1, gw), lambda b, c, g: (0, g)),
            pl.BlockSpec((1, ns), lambda b, c, g: (0, g)),
            pl.BlockSpec((1, ns), lambda b, c, g: (0, g)),
            pl.BlockSpec((None, hpg, 1), lambda b, c, g: (g, 0, 0)),
            pl.BlockSpec((None, hpg, 1), lambda b, c, g: (g, 0, 0)),
            pl.BlockSpec((1, gw), lambda b, c, g: (0, g)),
            pl.BlockSpec((1, gw), lambda b, c, g: (0, g)),
        ],
        out_specs=pl.BlockSpec((None, ch, gw), grp),
        out_shape=jax.ShapeDtypeStruct((bsz, n_rows, di), BF16),
        scratch_shapes=[
            pltpu.VMEM((ng, ns, gw), F32),
            pltpu.VMEM((ng, 8, ew), F32),
            pltpu.VMEM((16, ew), F32),
            pltpu.VMEM((cw - 1, ch, ch), BF16),
            pltpu.VMEM((2, ch, ch), BF16),
        ],
        compiler_params=_params("arbitrary", "arbitrary", "arbitrary"),
        name="ssd",
    )(main3, main3, main3, main3, tail_t, cwx, cwb, cwc, cbx, cbb, cbc,
      dt_bias.reshape(ng, hpg, 1), a_log.reshape(ng, hpg, 1),
      jnp.repeat(d_skip, cfg.ssm_head_dim).reshape(1, di), norm_w.reshape(1, di))


def _merge_kernel(attn_ref, ssm_ref, ga_ref, gs_ref, wa_ref, ws_ref, o_ref):
    pa = jnp.dot(attn_ref[...], wa_ref[...], preferred_element_type=F32)
    ps = jnp.dot(ssm_ref[...], ws_ref[...], preferred_element_type=F32)
    ga = jax.nn.sigmoid(ga_ref[...].astype(F32))
    gs = jax.nn.sigmoid(gs_ref[...].astype(F32))
    o_ref[...] = (ga * pa + gs * ps).astype(o_ref.dtype)


def _merge(cfg, attn, ssm, main, w_a, w_s, bsz, n_rows):
    ch, d = cfg.chunk, cfg.d_model
    nq = n_rows // ch - 1
    tn = _pick(d, (1024, 512, 256))
    main3 = main.reshape(bsz, n_rows, -1)
    g_off = cfg.qw + 2 * cfg.kvw + cfg.iw + 2 * cfg.d_inner + 2 * cfg.bcw
    assert g_off % tn == 0
    return pl.pallas_call(
        _merge_kernel,
        grid=(d // tn, bsz, nq),
        in_specs=[
            pl.BlockSpec((None, ch, cfg.qw), lambda j, b, i: (b, i, 0)),
            pl.BlockSpec((None, ch, cfg.d_inner), lambda j, b, i: (b, i + 1, 0)),
            pl.BlockSpec((None, ch, tn), lambda j, b, i: (b, i + 1, g_off // tn + j)),
            pl.BlockSpec((None, ch, tn), lambda j, b, i: (b, i + 1, (g_off + d) // tn + j)),
            pl.BlockSpec((cfg.qw, tn), lambda j, b, i: (0, j)),
            pl.BlockSpec((cfg.d_inner, tn), lambda j, b, i: (0, j)),
        ],
        out_specs=pl.BlockSpec((None, ch, tn), lambda j, b, i: (b, i, j)),
        out_shape=jax.ShapeDtypeStruct((bsz, nq * ch, d), BF16),
        compiler_params=_params("parallel", "parallel", "parallel"),
        name="merge",
    )(attn, ssm, main3, main3, w_a, w_s)


def _outproj_kernel(eps, m_ref, h_ref, w_ref, nw_ref, h2_ref, hn_ref, hnt_ref):
    h2 = h_ref[...] + jnp.dot(m_ref[...], w_ref[...], preferred_element_type=F32)
    h2_ref[...] = h2
    ms = jnp.mean(h2 * h2, axis=-1, keepdims=True)
    hn = h2 * lax.rsqrt(ms + eps) * nw_ref[...]
    hn_ref[...] = hn.astype(hn_ref.dtype)
    hnt_ref[...] = hn.T.astype(hnt_ref.dtype)


def _outproj(cfg, merged, x, w_out, norm_w, bsz, n_rows):
    ch, d = cfg.chunk, cfg.d_model
    nq = n_rows // ch - 1
    return pl.pallas_call(
        functools.partial(_outproj_kernel, cfg.eps),
        grid=(bsz, nq),
        in_specs=[
            pl.BlockSpec((None, ch, d), lambda b, i: (b, i, 0)),
            pl.BlockSpec((None, ch, d), lambda b, i: (b, i, 0)),
            pl.BlockSpec((d, d), lambda b, i: (0, 0)),
            pl.BlockSpec((1, d), lambda b, i: (0, 0)),
        ],
        out_specs=[
            pl.BlockSpec((None, ch, d), lambda b, i: (b, i, 0)),
            pl.BlockSpec((None, ch, d), lambda b, i: (b, i, 0)),
            pl.BlockSpec((d, ch), lambda b, i: (0, b * nq + i)),
        ],
        out_shape=[
            jax.ShapeDtypeStruct((bsz, nq * ch, d), F32),
            jax.ShapeDtypeStruct((bsz, nq * ch, d), BF16),
            jax.ShapeDtypeStruct((d, bsz * nq * ch), BF16),
        ],
        compiler_params=_params("parallel", "parallel"),
        name="outproj",
    )(merged, x, w_out, norm_w.reshape(1, d))


def _topk_desc(s, k):
    out = []
    for _ in range(k):
        m = jnp.max(s, axis=0, keepdims=True)
        out.append(m)
        s = jnp.where(s >= m, -jnp.inf, s)
    return out


def _oddeven_merge_sort(n):
    pairs = []
    p = 1
    while p < n:
        k = p
        while k >= 1:
            for j in range(k % p, n - k, 2 * k):
                for i in range(min(k, n - j - k)):
                    if (i + j) // (2 * p) == (i + j + k) // (2 * p):
                        pairs.append((i + j, i + j + k))
            k //= 2
        p *= 2
    return pairs


def _top16_sorted(s):
    n = 16
    v = [s[r * 8:(r + 1) * 8, :] for r in range(n)]
    for i, j in _oddeven_merge_sort(n):
        v[i], v[j] = jnp.maximum(v[i], v[j]), jnp.minimum(v[i], v[j])
    for shift in (4, 2, 1):
        v = [jnp.maximum(v[r], pltpu.roll(v[n - 1 - r], shift, 0)) for r in range(n)]
        span = n // 2
        while span >= 1:
            for lo in range(n):
                if (lo // span) % 2 == 0:
                    hi = lo + span
                    v[lo], v[hi] = jnp.maximum(v[lo], v[hi]), jnp.minimum(v[lo], v[hi])
            span //= 2
    return v


def _route_kernel(cfg, x_ref, wq_ref, keys_ref, thr_ref, e0_ref, s1_ref, e1_ref):
    nk, k = cfg.n_keys, cfg.peer_topk
    half = cfg.peer_key_dim // 2
    q = jnp.dot(x_ref[...], wq_ref[...], preferred_element_type=F32)
    for h in range(cfg.peer_heads):
        st = []
        for cidx in range(2):
            off = (2 * h + cidx) * half
            qc = q[:, off:off + half].astype(BF16)
            st.append(lax.dot_general(keys_ref[cidx], qc, NT, preferred_element_type=F32))
        t0 = [t[0:1, :] for t in _top16_sorted(st[0])]
        t1 = [t[0:1, :] for t in _top16_sorted(st[1])]
        t1m = jnp.concatenate(t1, axis=0)
        cands = [a + t1m for a in t0]
        short = [cands[r][0:k // (r + 1), :] for r in range(k)]
        n_short = sum(k // (r + 1) for r in range(k))
        short.append(jnp.full((-n_short % 8, t1m.shape[1]), -jnp.inf, F32))
        best = _topk_desc(jnp.concatenate(short, axis=0), k)
        z = sum(jnp.exp(b - best[0]) for b in best)
        thr_i = jnp.full(st[0].shape, jnp.inf, F32)
        for r in range(k):
            lim = jnp.min(jnp.where(cands[r] >= best[k - 1], t1m, jnp.inf), axis=0, keepdims=True)
            thr_i = jnp.where(st[0] == t0[r], lim, thr_i)
        thr_ref[h] = thr_i
        e0_ref[h] = jnp.exp(st[0] - t0[0])
        s1_ref[h] = st[1]
        e1_ref[h] = jnp.exp(st[1] - t1[0]) / z


def _route(cfg, hn2, wq, keys):
    n, d = hn2.shape
    tm = 256
    nh, nk = cfg.peer_heads, cfg.n_keys
    qd = nh * cfg.peer_key_dim
    assert nk == LANES and cfg.peer_key_dim // 2 == LANES and cfg.peer_topk == 16
    big = pl.BlockSpec((nh, nk, tm), lambda i: (0, 0, i))
    shp = jax.ShapeDtypeStruct((nh, nk, n), F32)
    return pl.pallas_call(
        functools.partial(_route_kernel, cfg),
        grid=(n // tm,),
        in_specs=[
            pl.BlockSpec((tm, d), lambda i: (i, 0)),
            pl.BlockSpec((d, qd), lambda i: (0, 0)),
            pl.BlockSpec((2, nk, cfg.peer_key_dim // 2), lambda i: (0, 0, 0)),
        ],
        out_specs=[big, big, big, big],
        out_shape=[shp, shp, shp, shp],
        compiler_params=_params("parallel"),
        name="peer_route",
    )(hn2, wq, keys)


def _gelu(x):
    return 0.5 * x * (1.0 + lax.erf(x * (2.0 ** -0.5)))


def _gate_tile(cfg, i, lanes, thr_ref, e0_ref, s1_ref, e1_ref):
    w = None
    for h in range(cfg.peer_heads):
        sel = s1_ref[h, :, lanes] >= thr_ref[h, i:i + 1, lanes]
        term = jnp.where(sel, e1_ref[h, :, lanes], 0.0) * e0_ref[h, i:i + 1, lanes]
        w = term if w is None else w + term
    return w


MXU_ROWS = 256


def _peer_kernel(cfg, ic, n_e, xt_ref, u_ref, vt_ref, thr_ref, e0_ref, s1_ref, e1_ref, o_ref, a_ref, hb_ref):
    nk = cfg.n_keys
    d, tm = o_ref.shape
    ec = ic * nk
    s = pl.program_id(0)
    e_down = jnp.maximum(s - 2, 0) % n_e
    slot = s % 2

    @pl.when(s == 0)
    def _():
        a_ref[1] = jnp.zeros(a_ref.shape[1:], F32)
        hb_ref[...] = jnp.zeros(hb_ref.shape, BF16)

    @pl.when(e_down == 0)
    def _():
        o_ref[...] = jnp.zeros(o_ref.shape, F32)

    def up(m0):
        rows = slice(m0, m0 + MXU_ROWS)
        a_ref[slot, rows, :] = jnp.dot(u_ref[rows, :], xt_ref[...], preferred_element_type=F32)

    def down(m0):
        rows = slice(m0, m0 + MXU_ROWS)
        o_ref[rows, :] += jnp.dot(vt_ref[rows, :], hb_ref[slot], preferred_element_type=F32)

    def gate(i, c):
        rows, lanes = slice(i * nk, (i + 1) * nk), slice(c * LANES, (c + 1) * LANES)
        w = _gate_tile(cfg, i, lanes, thr_ref, e0_ref, s1_ref, e1_ref)
        hb_ref[1 - slot, rows, lanes] = (w * _gelu(a_ref[1 - slot, rows, lanes])).astype(BF16)

    mxu = [functools.partial(up, m0) for m0 in range(0, ec, MXU_ROWS)]
    mxu += [functools.partial(down, m0) for m0 in range(0, d, MXU_ROWS)]
    vec = [functools.partial(gate, i, c) for i in range(ic) for c in range(tm // LANES)]
    done = 0
    for k, piece in enumerate(mxu):
        piece()
        upto = (k + 1) * len(vec) // len(mxu)
        for tile in vec[done:upto]:
            tile()
        done = upto


def _peer(cfg, hn2t, u_b, v, routes):
    d, n = hn2t.shape
    nh, nk = cfg.peer_heads, cfg.n_keys
    tm = _pick(n, (512, 256))
    ic = 8
    ec = ic * nk
    n_e = nk // ic
    vt_b = v.reshape(n_e, ec, d).transpose(0, 2, 1).astype(BF16)
    n_steps = (n // tm) * n_e
    assert ec % MXU_ROWS == 0 and d % MXU_ROWS == 0
    back = lambda s, k: jnp.clip(s - k, 0, n_steps - 1)
    chunked = pl.BlockSpec((nh, ic, tm), lambda s: (0, back(s, 1) % n_e, back(s, 1) // n_e))
    full = pl.BlockSpec((nh, nk, tm), lambda s: (0, 0, back(s, 1) // n_e))
    return pl.pallas_call(
        functools.partial(_peer_kernel, cfg, ic, n_e),
        grid=(n_steps + 2,),
        in_specs=[
            pl.BlockSpec((d, tm), lambda s: (0, back(s, 0) // n_e)),
            pl.BlockSpec((ec, d), lambda s: (back(s, 0) % n_e, 0)),
            pl.BlockSpec((None, d, ec), lambda s: (back(s, 2) % n_e, 0, 0)),
            chunked, chunked, full, full,
        ],
        out_specs=pl.BlockSpec((d, tm), lambda s: (0, back(s, 2) // n_e)),
        out_shape=jax.ShapeDtypeStruct((d, n), F32),
        scratch_shapes=[pltpu.VMEM((2, ec, tm), F32), pltpu.VMEM((2, ec, tm), BF16)],
        compiler_params=_params("arbitrary"),
        name="peer_experts",
    )(hn2t, u_b, vt_b, *routes)


def _final_kernel(eps, h2_ref, pt_ref, nw_ref, o_ref):
    h = h2_ref[...] + pt_ref[...].T
    ms = jnp.mean(h * h, axis=-1, keepdims=True)
    o_ref[...] = h * lax.rsqrt(ms + eps) * nw_ref[...]


def _final(cfg, h2, peer_t, norm_w):
    n, d = h2.shape
    tm = 256
    return pl.pallas_call(
        functools.partial(_final_kernel, cfg.eps),
        grid=(n // tm,),
        in_specs=[
            pl.BlockSpec((tm, d), lambda i: (i, 0)),
            pl.BlockSpec((d, tm), lambda i: (0, i)),
            pl.BlockSpec((1, d), lambda i: (0, 0)),
        ],
        out_specs=pl.BlockSpec((tm, d), lambda i: (i, 0)),
        out_shape=jax.ShapeDtypeStruct((n, d), F32),
        compiler_params=_params("parallel"),
        name="final_norm",
    )(h2, peer_t, norm_w.reshape(1, d))


def _layer(cfg, x, meta_tokens, norm_mix_w, w_in, conv_w, conv_b, dt_bias, a_log, d_skip, ssm_norm_w,
           w_branch_attn, w_branch_ssm, w_out, norm_ffn_w, peer_w_query, peer_sub_keys, peer_u, peer_v,
           norm_final_w):
    bsz, seq, d = x.shape
    ch = cfg.chunk
    assert seq % ch == 0 and norm_mix_w.shape[0] == 1
    n_rows = ch + seq
    topk = min(cfg.topk_max, seq // 4)
    w_t = w_in[0].T
    o_iq = cfg.qw + 2 * cfg.kvw
    o_ik = o_iq + cfg.iw
    o_iw = o_ik + cfg.idx_dim
    o_z = o_iw + cfg.idx_heads
    o_dt = o_z + 2 * cfg.d_inner + 2 * cfg.bcw
    o_g = o_dt + cfg.ssm_heads
    segments = ((0, o_ik), (o_z, o_dt - o_z), (o_g, w_t.shape[0] - o_g))
    zpad = lambda n: jnp.zeros((n, d), w_t.dtype)
    w_tail_t = jnp.concatenate([
        w_t[o_ik:o_z], zpad(TAIL_DT - cfg.idx_dim - cfg.idx_heads),
        w_t[o_dt:o_g], zpad(TAIL_W - TAIL_DT - cfg.ssm_heads)], axis=0)

    hn = _prenorm(cfg, x, meta_tokens, norm_mix_w[0])
    main, tail = _inproj(cfg, hn.reshape(bsz * n_rows, d), w_t, w_tail_t, segments)
    qk_r, qi_r, ki2, vt, tail_t = _rope(cfg, main, tail, _rope_tables(cfg, n_rows), bsz, n_rows)
    attn = _dsa(cfg, topk, qk_r, qi_r, ki2, vt, tail_t, bsz, n_rows)
    ssm = _ssd(cfg, main, tail_t, conv_w[0], conv_b[0], dt_bias[0], a_log[0], d_skip[0], ssm_norm_w[0], bsz, n_rows)
    merged = _merge(cfg, attn, ssm, main, w_branch_attn[0].astype(BF16), w_branch_ssm[0].astype(BF16), bsz, n_rows)
    h2, hn2, hn2t = _outproj(cfg, merged, x, w_out[0].astype(BF16), norm_ffn_w[0], bsz, n_rows)
    h2 = h2.reshape(bsz * seq, d)
    hn2 = hn2.reshape(bsz * seq, d)
    routes = _route(cfg, hn2, peer_w_query[0].astype(BF16), peer_sub_keys[0].astype(BF16))
    peer_t = _peer(cfg, hn2t, peer_u[0].astype(BF16), peer_v[0], routes)
    out = _final(cfg, h2, peer_t, norm_final_w)
    return out.reshape(bsz, seq, d)


def kernel(x, meta_tokens, norm_mix_w, w_in, conv_w, conv_b, dt_bias, a_log, d_skip, ssm_norm_w, w_branch_attn,
           w_branch_ssm, w_out, norm_ffn_w, peer_w_query, peer_sub_keys, peer_u, peer_v, norm_final_w):
    return _layer(Cfg(), x, meta_tokens, norm_mix_w, w_in, conv_w, conv_b, dt_bias, a_log, d_skip, ssm_norm_w,
                  w_branch_attn, w_branch_ssm, w_out, norm_ffn_w, peer_w_query, peer_sub_keys, peer_u, peer_v,
                  norm_final_w)
```
